```python
import math
import jax, jax.numpy as jnp
from jax import lax
import numpy as np

D_MODEL = 1024
BATCH = 8
SEQ = 4096
DEPTH = 4

BRANCH_WIDTH = 512
N_BRANCHES = 3
SWA_Q_HEADS = 8
SWA_KV_HEADS = 2
SWA_GROUP = SWA_Q_HEADS // SWA_KV_HEADS
SWA_HEAD_DIM = BRANCH_WIDTH // SWA_Q_HEADS
WINDOW = 128
N_BUCKETS = 32
MAX_DISTANCE = 128
CONV_CHANNELS = BRANCH_WIDTH
CONV_WIDTH = 31
MLA_HEADS = 8
MLA_Q_RANK = 256
MLA_KV_RANK = 128
MLA_NOPE_DIM = 64
MLA_ROPE_DIM = 32
MLA_V_DIM = BRANCH_WIDTH // MLA_HEADS
ROPE_THETA = 10000.0
Q_BLOCK = 128
D_FF = 4 * D_MODEL
EPS = 1e-6
NEG_INF = -1e30

IN_SPLIT_SIZES = (
    SWA_Q_HEADS * SWA_HEAD_DIM,
    SWA_KV_HEADS * SWA_HEAD_DIM,
    SWA_KV_HEADS * SWA_HEAD_DIM,
    2 * CONV_CHANNELS,
    MLA_Q_RANK,
    MLA_KV_RANK,
    MLA_ROPE_DIM,
    N_BRANCHES * D_MODEL,
)
IN_COLS = sum(IN_SPLIT_SIZES)
IN_SPLIT_POINTS = tuple(sum(IN_SPLIT_SIZES[: i + 1]) for i in range(len(IN_SPLIT_SIZES) - 1))

kernel_name = "hybrid_swa_conformer_mla_gated_block"


def rms_norm(x, g):
    xf = x.astype(jnp.float32)
    y = xf * lax.rsqrt(jnp.mean(xf * xf, axis=-1, keepdims=True) + EPS)
    return (y * g.astype(jnp.float32)).astype(x.dtype)


def layer_norm(x, g, b):
    xf = x.astype(jnp.float32)
    mu = jnp.mean(xf, axis=-1, keepdims=True)
    var = jnp.mean(jnp.square(xf - mu), axis=-1, keepdims=True)
    y = (xf - mu) * lax.rsqrt(var + EPS) * g.astype(jnp.float32) + b.astype(jnp.float32)
    return y.astype(x.dtype)


def t5_bucket(rel):
    n = jnp.maximum(rel, 0)
    max_exact = N_BUCKETS // 2
    nf = jnp.maximum(n, max_exact).astype(jnp.float32)
    large = max_exact + (jnp.log(nf / max_exact) / math.log(MAX_DISTANCE / max_exact)
                         * (N_BUCKETS - max_exact)).astype(jnp.int32)
    return jnp.where(n < max_exact, n, jnp.minimum(large, N_BUCKETS - 1))


def band_frames(t):
    b, s = t.shape[:2]
    blocks = t.reshape((b, s // WINDOW, WINDOW) + t.shape[2:])
    prev = jnp.concatenate([jnp.zeros_like(blocks[:, :1]), blocks[:, :-1]], axis=1)
    return jnp.concatenate([prev, blocks], axis=2)


def swa_bias_and_mask(positions, rel_bias):
    b, s = positions.shape
    nb = s // WINDOW
    pq = positions.reshape(b, nb, WINDOW)
    pk = band_frames(positions)
    bucket = t5_bucket(pq[..., :, None] - pk[..., None, :])
    bias = rel_bias[bucket].astype(jnp.float32).reshape(b, nb, WINDOW, 2 * WINDOW, SWA_KV_HEADS, SWA_GROUP)
    qi = WINDOW + jnp.arange(WINDOW)[:, None]
    ki = jnp.arange(2 * WINDOW)[None, :]
    in_band = (ki <= qi) & (qi - ki < WINDOW)
    has_prev = (jnp.arange(nb) > 0)[:, None, None] | (ki >= WINDOW)[None]
    mask = in_band[None] & has_prev
    return bias, mask[None, :, :, :, None, None]


def swa_attention(q, k, v, sinks, bias, mask):
    b, s = q.shape[:2]
    nb = s // WINDOW
    qb = q.reshape(b, nb, WINDOW, SWA_KV_HEADS, SWA_GROUP, SWA_HEAD_DIM)
    kk = band_frames(k.reshape(b, s, SWA_KV_HEADS, SWA_HEAD_DIM))
    vv = band_frames(v.reshape(b, s, SWA_KV_HEADS, SWA_HEAD_DIM))
    scores = jnp.einsum('bnqhgd,bnkhd->bnqkhg', qb, kk).astype(jnp.float32) * (SWA_HEAD_DIM ** -0.5) + bias
    scores = jnp.where(mask, scores, NEG_INF)
    sink = sinks.astype(jnp.float32).reshape(SWA_KV_HEADS, SWA_GROUP)
    m = jnp.maximum(scores.max(axis=3, keepdims=True), sink)
    p = jnp.exp(scores - m)
    p = p / (p.sum(axis=3, keepdims=True) + jnp.exp(sink - m))
    out = jnp.einsum('bnqkhg,bnkhd->bnqhgd', p.astype(vv.dtype), vv)
    return out.reshape(b, s, SWA_Q_HEADS * SWA_HEAD_DIM)


def conformer_conv(u_glu, w_dw, b_dw, g_ln, b_ln):
    a, gate = jnp.split(u_glu, 2, axis=-1)
    u = a * jax.nn.sigmoid(gate)
    u = lax.conv_general_dilated(u, w_dw, window_strides=(1,), padding=[(CONV_WIDTH - 1, 0)],
                                 dimension_numbers=('NWC', 'WIO', 'NWC'),
                                 feature_group_count=CONV_CHANNELS) + b_dw
    return jax.nn.silu(layer_norm(u, g_ln, b_ln))


def rope(x, positions):
    d = x.shape[-1]
    half = d // 2
    freqs = jnp.exp(-math.log(ROPE_THETA) * jnp.arange(half, dtype=jnp.float32) / half)
    ang = positions.astype(jnp.float32)[..., None] * freqs
    cos = jnp.cos(ang)[:, :, None, :]
    sin = jnp.sin(ang)[:, :, None, :]
    x1 = x[..., :half].astype(jnp.float32)
    x2 = x[..., half:].astype(jnp.float32)
    return jnp.concatenate([x1 * cos - x2 * sin, x2 * cos + x1 * sin], axis=-1).astype(x.dtype)


def mla_attention(cq, ckv, kpe_raw, positions, g_qn, w_uq, g_kvn, w_ukv):
    b, s = cq.shape[:2]
    q = (rms_norm(cq, g_qn) @ w_uq).reshape(b, s, MLA_HEADS, MLA_NOPE_DIM + MLA_ROPE_DIM)
    q_nope = q[..., :MLA_NOPE_DIM]
    q_pe = rope(q[..., MLA_NOPE_DIM:], positions)
    kv = (rms_norm(ckv, g_kvn) @ w_ukv).reshape(b, s, MLA_HEADS, MLA_NOPE_DIM + MLA_V_DIM)
    k_nope = kv[..., :MLA_NOPE_DIM]
    v = kv[..., MLA_NOPE_DIM:]
    k_pe = rope(kpe_raw[:, :, None, :], positions)[:, :, 0]
    scale = (MLA_NOPE_DIM + MLA_ROPE_DIM) ** -0.5
    nb = s // Q_BLOCK
    key_idx = jnp.arange(s)

    def to_blocks(t):
        return jnp.moveaxis(t.reshape((b, nb, Q_BLOCK) + t.shape[2:]), 1, 0)

    def attend_block(args):
        qn, qp, blk = args
        sc = (jnp.einsum('bqhd,bkhd->bhqk', qn, k_nope)
              + jnp.einsum('bqhd,bkd->bhqk', qp, k_pe)).astype(jnp.float32) * scale
        q_idx = blk * Q_BLOCK + jnp.arange(Q_BLOCK)
        sc = jnp.where(key_idx[None, :] <= q_idx[:, None], sc, NEG_INF)
        p = jax.nn.softmax(sc, axis=-1)
        return jnp.einsum('bhqk,bkhd->bqhd', p.astype(v.dtype), v)

    out = lax.map(attend_block, (to_blocks(q_nope), to_blocks(q_pe), jnp.arange(nb)))
    return jnp.moveaxis(out, 0, 1).reshape(b, s, MLA_HEADS * MLA_V_DIM)


def setup_inputs(seed: int = 0) -> dict:
    key = jax.random.key(seed)
    ks = jax.random.split(key, 20)
    f32 = jnp.float32

    def nrm(k, shape, scale):
        return jax.random.normal(k, shape, f32) * scale

    def gain(k, shape):
        return 1.0 + 0.05 * jax.random.normal(k, shape, f32)

    return {
        "x": nrm(ks[0], (BATCH, SEQ, D_MODEL), 1.0),
        "positions": jnp.broadcast_to(jnp.arange(SEQ, dtype=jnp.int32), (BATCH, SEQ)),
        "rel_bias": nrm(ks[1], (N_BUCKETS, SWA_Q_HEADS), 0.5),
        "g_final": gain(ks[2], (D_MODEL,)),
        "g_mix": gain(ks[3], (DEPTH, D_MODEL)),
        "w_in": nrm(ks[4], (DEPTH, D_MODEL, IN_COLS), D_MODEL ** -0.5),
        "swa_sinks": nrm(ks[5], (DEPTH, SWA_Q_HEADS), 0.5),
        "g_q_norm": gain(ks[6], (DEPTH, MLA_Q_RANK)),
        "w_q_up": nrm(ks[7], (DEPTH, MLA_Q_RANK, MLA_HEADS * (MLA_NOPE_DIM + MLA_ROPE_DIM)), MLA_Q_RANK ** -0.5),
        "g_kv_norm": gain(ks[8], (DEPTH, MLA_KV_RANK)),
        "w_kv_up": nrm(ks[9], (DEPTH, MLA_KV_RANK, MLA_HEADS * (MLA_NOPE_DIM + MLA_V_DIM)), MLA_KV_RANK ** -0.5),
        "w_dw": nrm(ks[10], (DEPTH, CONV_WIDTH, 1, CONV_CHANNELS), CONV_WIDTH ** -0.5),
        "b_dw": nrm(ks[11], (DEPTH, CONV_CHANNELS), 0.02),
        "g_conv_ln": gain(ks[12], (DEPTH, CONV_CHANNELS)),
        "b_conv_ln": nrm(ks[13], (DEPTH, CONV_CHANNELS), 0.02),
        "w_branch": nrm(ks[14], (DEPTH, N_BRANCHES, BRANCH_WIDTH, D_MODEL), BRANCH_WIDTH ** -0.5),
        "w_out": nrm(ks[15], (DEPTH, D_MODEL, D_MODEL), D_MODEL ** -0.5),
        "g_mlp": gain(ks[16], (DEPTH, D_MODEL)),
        "w_up": nrm(ks[17], (DEPTH, D_MODEL, D_FF), D_MODEL ** -0.5),
        "w_down": nrm(ks[18], (DEPTH, D_FF, D_MODEL), D_FF ** -0.5),
    }


def reference(x, positions, rel_bias, g_final, g_mix, w_in, swa_sinks, g_q_norm, w_q_up,
              g_kv_norm, w_kv_up, w_dw, b_dw, g_conv_ln, b_conv_ln, w_branch, w_out,
              g_mlp, w_up, w_down):
    b, s, _ = x.shape
    swa_bias, swa_mask = swa_bias_and_mask(positions, rel_bias)
    for l in range(DEPTH):
        h = rms_norm(x, g_mix[l])
        q_a, k_a, v_a, u_b, cq_c, ckv_c, kpe_c, gate_logits = jnp.split(h @ w_in[l], IN_SPLIT_POINTS, axis=-1)
        y_a = swa_attention(q_a, k_a, v_a, swa_sinks[l], swa_bias, swa_mask)
        y_b = conformer_conv(u_b, w_dw[l], b_dw[l], g_conv_ln[l], b_conv_ln[l])
        y_c = mla_attention(cq_c, ckv_c, kpe_c, positions, g_q_norm[l], w_q_up[l],
                            g_kv_norm[l], w_kv_up[l])
        branches = jnp.einsum('bsnc,ncd->bsnd', jnp.stack([y_a, y_b, y_c], axis=2), w_branch[l])
        gates = jax.nn.sigmoid(gate_logits.reshape(b, s, N_BRANCHES, D_MODEL))
        x = x + jnp.einsum('bsnd,bsnd->bsd', gates, branches) @ w_out[l]
        h = rms_norm(x, g_mlp[l])
        x = x + jnp.square(jax.nn.relu(h @ w_up[l])) @ w_down[l]
    return rms_norm(x, g_final)
```

```python
import functools
import math

import jax
import jax.numpy as jnp
from jax import lax
from jax.experimental import pallas as pl
from jax.experimental.pallas import tpu as pltpu

D_MODEL = 1024
DEPTH = 4
BRANCH_WIDTH = 512
N_BRANCHES = 3
SWA_Q_HEADS = 8
SWA_KV_HEADS = 2
SWA_GROUP = SWA_Q_HEADS // SWA_KV_HEADS
SWA_HEAD_DIM = 64
WINDOW = 128
N_BUCKETS = 32
MAX_DISTANCE = 128
CONV_CHANNELS = 512
CONV_WIDTH = 31
MLA_HEADS = 8
MLA_Q_RANK = 256
MLA_KV_RANK = 128
MLA_NOPE_DIM = 64
MLA_ROPE_DIM = 32
MLA_V_DIM = 64
ROPE_THETA = 10000.0
D_FF = 4 * D_MODEL
EPS = 1e-6
NEG_INF = -1e30

LANES = 128
SUBLANES = 8
VMEM_LIMIT = 56 * 1024 * 1024

F32 = jnp.float32
BF16 = jnp.bfloat16

_MAX_EXACT = N_BUCKETS // 2
_T5_THRESHOLDS = tuple(range(1, _MAX_EXACT + 1)) + tuple(
    math.ceil(_MAX_EXACT * (MAX_DISTANCE / _MAX_EXACT) ** (j / (N_BUCKETS - _MAX_EXACT)))
    for j in range(1, N_BUCKETS - _MAX_EXACT))

_QKV_W = 768
_U_W = 2 * CONV_CHANNELS
_CMLA_W = MLA_Q_RANK + MLA_KV_RANK + 2 * LANES
_GATE_W = N_BRANCHES * D_MODEL
_IN_W = _QKV_W + _U_W + _CMLA_W + _GATE_W


def _params(*semantics):
    return pltpu.CompilerParams(dimension_semantics=semantics, vmem_limit_bytes=VMEM_LIMIT)


def _rms(xf, g):
    return xf * lax.rsqrt(jnp.mean(xf * xf, axis=-1, keepdims=True) + EPS) * g


def _bias_kernel(posq_ref, pkp_ref, pkc_ref, tbl_ref, out_ref):
    blk = pl.program_id(1)
    pk_prev = jnp.where(blk > 0, pkp_ref[0], 0)
    pk = jnp.concatenate([pk_prev, pkc_ref[0]], axis=1)
    ki = lax.broadcasted_iota(jnp.int32, (SUBLANES, 2 * WINDOW), 1)
    has_prev = (blk > 0) | (ki >= WINDOW)

    def rows(r, carry):
        r0 = pl.multiple_of(r * SUBLANES, SUBLANES)
        pq = posq_ref[0, pl.ds(r0, SUBLANES), :]
        n = jnp.maximum(pq - pk, 0)
        qi = WINDOW + r0 + lax.broadcasted_iota(jnp.int32, (SUBLANES, 2 * WINDOW), 0)
        mask = (ki <= qi) & (qi - ki < WINDOW) & has_prev
        vals = [jnp.broadcast_to(tbl_ref[pl.ds(h, 1), :], n.shape) for h in range(SWA_Q_HEADS)]
        for b in range(1, N_BUCKETS):
            ge = n >= _T5_THRESHOLDS[b - 1]
            for h in range(SWA_Q_HEADS):
                t = jnp.broadcast_to(tbl_ref[pl.ds(b * SWA_Q_HEADS + h, 1), :], n.shape)
                vals[h] = jnp.where(ge, t, vals[h])
        for h in range(SWA_Q_HEADS):
            out_ref[0, 0, h, pl.ds(r0, SUBLANES), :] = jnp.where(mask, vals[h], NEG_INF)
        return carry

    lax.fori_loop(0, WINDOW // SUBLANES, rows, 0)


def _swa_bias(positions, rel_bias):
    b, s = positions.shape
    nb = s // WINDOW
    tbl = jnp.broadcast_to(rel_bias.astype(F32).reshape(N_BUCKETS * SWA_Q_HEADS, 1),
                           (N_BUCKETS * SWA_Q_HEADS, 2 * WINDOW))
    pos_col = positions.reshape(b, s, 1)
    pos_row = positions.reshape(b, 1, s)
    return pl.pallas_call(
        _bias_kernel,
        grid=(b, nb),
        in_specs=[
            pl.BlockSpec((1, WINDOW, 1), lambda i, n: (i, n, 0)),
            pl.BlockSpec((1, 1, WINDOW), lambda i, n: (i, 0, jnp.maximum(n - 1, 0))),
            pl.BlockSpec((1, 1, WINDOW), lambda i, n: (i, 0, n)),
            pl.BlockSpec((N_BUCKETS * SWA_Q_HEADS, 2 * WINDOW), lambda i, n: (0, 0)),
        ],
        out_specs=pl.BlockSpec((1, 1, SWA_Q_HEADS, WINDOW, 2 * WINDOW), lambda i, n: (i, n, 0, 0, 0)),
        out_shape=jax.ShapeDtypeStruct((b, nb, SWA_Q_HEADS, WINDOW, 2 * WINDOW), F32),
        compiler_params=_params("parallel", "parallel"),
        name="swa_bias",
    )(pos_col, pos_row, pos_row, tbl)


def _in_proj_kernel(x_ref, g_ref, w_ref, qkv_ref, u_ref, c_ref, gate_ref, *, chunk):
    h = _rms(x_ref[...], g_ref[...]).astype(BF16)
    col = 0
    for out_ref in (qkv_ref, u_ref, c_ref, gate_ref):
        width = out_ref.shape[1]
        for c0 in range(0, width, chunk):
            c1 = min(c0 + chunk, width)
            out_ref[:, c0:c1] = jnp.dot(h, w_ref[:, col + c0:col + c1],
                                        preferred_element_type=F32).astype(out_ref.dtype)
        col += width


def _in_proj(x2, g, w, tm=512):
    t = x2.shape[0]
    row = lambda i: (i, 0)
    const = lambda i: (0, 0)
    return pl.pallas_call(
        functools.partial(_in_proj_kernel, chunk=512),
        grid=(t // tm,),
        in_specs=[
            pl.BlockSpec((tm, D_MODEL), row),
            pl.BlockSpec((1, D_MODEL), const),
            pl.BlockSpec((D_MODEL, _IN_W), const),
        ],
        out_specs=[
            pl.BlockSpec((tm, _QKV_W), row),
            pl.BlockSpec((tm, _U_W), row),
            pl.BlockSpec((tm, _CMLA_W), row),
            pl.BlockSpec((tm, _GATE_W), row),
        ],
        out_shape=[
            jax.ShapeDtypeStruct((t, _QKV_W), BF16),
            jax.ShapeDtypeStruct((t, _U_W), F32),
            jax.ShapeDtypeStruct((t, _CMLA_W), F32),
            jax.ShapeDtypeStruct((t, _GATE_W), F32),
        ],
        compiler_params=_params("parallel"),
        name="in_proj",
    )(x2, g, w)


def _swa_kernel(q_ref, kp_ref, kc_ref, vp_ref, vc_ref, bias_ref, sink_ref, o_ref):
    k = jnp.concatenate([kp_ref[0], kc_ref[0]], axis=0)
    v = jnp.concatenate([vp_ref[0], vc_ref[0]], axis=0)
    scale = SWA_HEAD_DIM ** -0.5
    outs = []
    for h in range(SWA_Q_HEADS):
        hk = h // SWA_GROUP
        qh = q_ref[0, :, h * SWA_HEAD_DIM:(h + 1) * SWA_HEAD_DIM]
        kh = k[:, hk * SWA_HEAD_DIM:(hk + 1) * SWA_HEAD_DIM]
        vh = v[:, hk * SWA_HEAD_DIM:(hk + 1) * SWA_HEAD_DIM]
        s = lax.dot_general(qh, kh, (((1,), (1,)), ((), ())), preferred_element_type=F32)
        s = s * scale + bias_ref[0, 0, h]
        sink = sink_ref[h]
        m = jnp.maximum(jnp.max(s, axis=1, keepdims=True), sink)
        p = jnp.exp(s - m)
        denom = jnp.sum(p, axis=1, keepdims=True) + jnp.exp(sink - m)
        o = jnp.dot(p.astype(BF16), vh, preferred_element_type=F32)
        outs.append(o / denom)
    o_ref[0] = jnp.concatenate(outs, axis=1).astype(o_ref.dtype)


def _swa(qkv, bias, sinks):
    b, s, _ = qkv.shape
    nb = s // WINDOW
    kcol = (SWA_Q_HEADS * SWA_HEAD_DIM) // LANES
    vcol = kcol + 1
    prev = lambda n: jnp.maximum(n - 1, 0)
    return pl.pallas_call(
        _swa_kernel,
        grid=(b, nb),
        in_specs=[
            pl.BlockSpec((1, WINDOW, SWA_Q_HEADS * SWA_HEAD_DIM), lambda i, n: (i, n, 0)),
            pl.BlockSpec((1, WINDOW, LANES), lambda i, n: (i, prev(n), kcol)),
            pl.BlockSpec((1, WINDOW, LANES), lambda i, n: (i, n, kcol)),
            pl.BlockSpec((1, WINDOW, LANES), lambda i, n: (i, prev(n), vcol)),
            pl.BlockSpec((1, WINDOW, LANES), lambda i, n: (i, n, vcol)),
            pl.BlockSpec((1, 1, SWA_Q_HEADS, WINDOW, 2 * WINDOW), lambda i, n: (i, n, 0, 0, 0)),
            pl.BlockSpec(memory_space=pltpu.SMEM),
        ],
        out_specs=pl.BlockSpec((1, WINDOW, BRANCH_WIDTH), lambda i, n: (i, n, 0)),
        out_shape=jax.ShapeDtypeStruct((b, s, BRANCH_WIDTH), BF16),
        compiler_params=_params("parallel", "parallel"),
        name="swa_attn",
    )(qkv, qkv, qkv, qkv, qkv, bias, sinks)


_CONV_HALO = 32
_CONV_CHUNK = 32


def _conv_kernel(uh_ref, uc_ref, w_ref, bdw_ref, g_ref, b_ref, o_ref, ext_ref, *, ts):
    c = CONV_CHANNELS

    def glu(u):
        return u[:, :c] * jax.nn.sigmoid(u[:, c:])

    halo = glu(uh_ref[0])
    ext_ref[0:_CONV_HALO, :] = jnp.where(pl.program_id(1) > 0, halo, 0.0)
    ext_ref[_CONV_HALO:, :] = glu(uc_ref[0])
    lead = _CONV_HALO - (CONV_WIDTH - 1)
    for r0 in range(0, ts, _CONV_CHUNK):
        acc = jnp.broadcast_to(bdw_ref[...], (_CONV_CHUNK, c))
        for j in range(CONV_WIDTH):
            start = r0 + lead + j
            acc = acc + w_ref[j:j + 1, :] * ext_ref[start:start + _CONV_CHUNK, :]
        mu = jnp.mean(acc, axis=-1, keepdims=True)
        d = acc - mu
        var = jnp.mean(d * d, axis=-1, keepdims=True)
        y = d * lax.rsqrt(var + EPS) * g_ref[...] + b_ref[...]
        o_ref[0, r0:r0 + _CONV_CHUNK, :] = (y * jax.nn.sigmoid(y)).astype(o_ref.dtype)


def _conv(u, w_dw, b_dw, g_ln, b_ln, ts=256):
    b, s, _ = u.shape
    ratio = ts // _CONV_HALO
    const = lambda i, n: (0, 0)
    return pl.pallas_call(
        functools.partial(_conv_kernel, ts=ts),
        grid=(b, s // ts),
        in_specs=[
            pl.BlockSpec((1, _CONV_HALO, _U_W), lambda i, n: (i, jnp.maximum(n * ratio - 1, 0), 0)),
            pl.BlockSpec((1, ts, _U_W), lambda i, n: (i, n, 0)),
            pl.BlockSpec((CONV_WIDTH, CONV_CHANNELS), const),
            pl.BlockSpec((1, CONV_CHANNELS), const),
            pl.BlockSpec((1, CONV_CHANNELS), const),
            pl.BlockSpec((1, CONV_CHANNELS), const),
        ],
        out_specs=pl.BlockSpec((1, ts, CONV_CHANNELS), lambda i, n: (i, n, 0)),
        out_shape=jax.ShapeDtypeStruct((b, s, CONV_CHANNELS), BF16),
        scratch_shapes=[pltpu.VMEM((ts + _CONV_HALO, CONV_CHANNELS), F32)],
        compiler_params=_params("parallel", "parallel"),
        name="conformer_conv",
    )(u, u, w_dw, b_dw, g_ln, b_ln)


def _mla_prep_kernel(c_ref, pos_ref, fpat_ref, sgn_ref, gq_ref, gkv_ref,
                     wq_ref, wqr_ref, wk_ref, wv_ref, q_ref, k_ref, v_ref):
    cq = c_ref[:, 0:MLA_Q_RANK]
    ckv = c_ref[:, MLA_Q_RANK:MLA_Q_RANK + MLA_KV_RANK]
    kpe_a = c_ref[:, MLA_Q_RANK + MLA_KV_RANK:MLA_Q_RANK + MLA_KV_RANK + LANES]
    kpe_b = c_ref[:, MLA_Q_RANK + MLA_KV_RANK + LANES:]
    ang = pos_ref[...].astype(F32) * fpat_ref[...]
    sgn = sgn_ref[...]
    cos = jnp.where(sgn != 0.0, jnp.cos(ang), 1.0)
    sin = jnp.sin(ang) * sgn
    hq = _rms(cq, gq_ref[...]).astype(BF16)
    hkv = _rms(ckv, gkv_ref[...]).astype(BF16)
    k_rope = kpe_a * cos + kpe_b * sin
    for h in range(MLA_HEADS):
        sl = slice(h * LANES, (h + 1) * LANES)
        qf = jnp.dot(hq, wq_ref[:, sl], preferred_element_type=F32)
        qr = jnp.dot(hq, wqr_ref[:, sl], preferred_element_type=F32)
        q_ref[:, sl] = (qf * cos + qr * sin).astype(q_ref.dtype)
        kn = jnp.dot(hkv, wk_ref[:, sl], preferred_element_type=F32)
        k_ref[:, sl] = (kn + k_rope).astype(k_ref.dtype)
    v_ref[...] = jnp.dot(hkv, wv_ref[...], preferred_element_type=F32).astype(v_ref.dtype)


def _mla_prep(cmla, pos_col, fpat, sgn, gq, gkv, wq, wqr, wk, wv, ts=512):
    t = cmla.shape[0]
    row = lambda i: (i, 0)
    const = lambda i: (0, 0)
    hw = MLA_HEADS * LANES
    return pl.pallas_call(
        _mla_prep_kernel,
        grid=(t // ts,),
        in_specs=[
            pl.BlockSpec((ts, _CMLA_W), row),
            pl.BlockSpec((ts, 1), row),
            pl.BlockSpec((1, LANES), const),
            pl.BlockSpec((1, LANES), const),
            pl.BlockSpec((1, MLA_Q_RANK), const),
            pl.BlockSpec((1, MLA_KV_RANK), const),
            pl.BlockSpec((MLA_Q_RANK, hw), const),
            pl.BlockSpec((MLA_Q_RANK, hw), const),
            pl.BlockSpec((MLA_KV_RANK, hw), const),
            pl.BlockSpec((MLA_KV_RANK, MLA_HEADS * MLA_V_DIM), const),
        ],
        out_specs=[
            pl.BlockSpec((ts, hw), row),
            pl.BlockSpec((ts, hw), row),
            pl.BlockSpec((ts, MLA_HEADS * MLA_V_DIM), row),
        ],
        out_shape=[
            jax.ShapeDtypeStruct((t, hw), BF16),
            jax.ShapeDtypeStruct((t, hw), BF16),
            jax.ShapeDtypeStruct((t, MLA_HEADS * MLA_V_DIM), BF16),
        ],
        compiler_params=_params("parallel"),
        name="mla_prep",
    )(cmla, pos_col, fpat, sgn, gq, gkv, wq, wqr, wk, wv)


def _mla_attn_kernel(q_ref, k_ref, v_ref, o_ref, *, tq, tk):
    i = pl.program_id(2)
    scale = (MLA_NOPE_DIM + MLA_ROPE_DIM) ** -0.5
    n_diag = tq // tk
    row = lax.broadcasted_iota(jnp.int32, (tq, tk), 0)
    col = lax.broadcasted_iota(jnp.int32, (tq, tk), 1)
    outs = []
    for hh in range(2):
        q = q_ref[0, :, hh * LANES:(hh + 1) * LANES]

        def step(j, carry, masked):
            m, l, acc = carry
            k0 = pl.multiple_of(j * tk, tk)
            k = k_ref[0, pl.ds(k0, tk), hh * LANES:(hh + 1) * LANES]
            v = v_ref[0, pl.ds(k0, tk), :]
            s = lax.dot_general(q, k, (((1,), (1,)), ((), ())), preferred_element_type=F32) * scale
            if masked:
                s = jnp.where(col + (j * tk - i * tq) <= row, s, NEG_INF)
            m_new = jnp.maximum(m, jnp.max(s, axis=1, keepdims=True))
            alpha = jnp.exp(m - m_new)
            p = jnp.exp(s - m_new)
            l = alpha * l + jnp.sum(p, axis=1, keepdims=True)
            acc = alpha * acc + jnp.dot(p.astype(BF16), v, preferred_element_type=F32)
            return m_new, l, acc

        carry = (jnp.full((tq, 1), NEG_INF, F32), jnp.zeros((tq, 1), F32), jnp.zeros((tq, LANES), F32))
        carry = lax.fori_loop(0, i * n_diag, functools.partial(step, masked=False), carry)
        for d in range(n_diag):
            carry = step(i * n_diag + d, carry, masked=True)
        _, l, acc = carry
        outs.append(acc / l)
    lane = lax.broadcasted_iota(jnp.int32, (tq, LANES), 1)
    o_ref[0] = jnp.where(lane < MLA_V_DIM, outs[0], outs[1]).astype(o_ref.dtype)


def _mla_attn(qf, kf, v, tq=256, tk=256):
    b, s, _ = qf.shape
    pairs = MLA_HEADS // 2
    return pl.pallas_call(
        functools.partial(_mla_attn_kernel, tq=tq, tk=tk),
        grid=(b, pairs, s // tq),
        in_specs=[
            pl.BlockSpec((1, tq, 2 * LANES), lambda i, p, n: (i, n, p)),
            pl.BlockSpec((1, s, 2 * LANES), lambda i, p, n: (i, 0, p)),
            pl.BlockSpec((1, s, LANES), lambda i, p, n: (i, 0, p)),
        ],
        out_specs=pl.BlockSpec((1, tq, LANES), lambda i, p, n: (i, n, p)),
        out_shape=jax.ShapeDtypeStruct((b, s, MLA_HEADS * MLA_V_DIM), BF16),
        compiler_params=_params("parallel", "parallel", "arbitrary"),
        name="mla_attn",
    )(qf, kf, v)


def _merge_kernel(ya_ref, yb_ref, yc_ref, gate_ref, x_ref, wb_ref, wo_ref, o_ref):
    merged = None
    for n, y_ref in enumerate((ya_ref, yb_ref, yc_ref)):
        br = jnp.dot(y_ref[...], wb_ref[n], preferred_element_type=F32)
        term = jax.nn.sigmoid(gate_ref[:, n * D_MODEL:(n + 1) * D_MODEL]) * br
        merged = term if merged is None else merged + term
    o_ref[...] = x_ref[...] + jnp.dot(merged.astype(BF16), wo_ref[...], preferred_element_type=F32)


def _merge(ya, yb, yc, gates, x2, wb, wo, tm=512):
    t = x2.shape[0]
    row = lambda i: (i, 0)
    return pl.pallas_call(
        _merge_kernel,
        grid=(t // tm,),
        in_specs=[
            pl.BlockSpec((tm, BRANCH_WIDTH), row),
            pl.BlockSpec((tm, BRANCH_WIDTH), row),
            pl.BlockSpec((tm, BRANCH_WIDTH), row),
            pl.BlockSpec((tm, _GATE_W), row),
            pl.BlockSpec((tm, D_MODEL), row),
            pl.BlockSpec((N_BRANCHES, BRANCH_WIDTH, D_MODEL), lambda i: (0, 0, 0)),
            pl.BlockSpec((D_MODEL, D_MODEL), lambda i: (0, 0)),
        ],
        out_specs=pl.BlockSpec((tm, D_MODEL), row),
        out_shape=jax.ShapeDtypeStruct((t, D_MODEL), F32),
        compiler_params=_params("parallel"),
        name="merge_out",
    )(ya, yb, yc, gates, x2, wb, wo)


def _ffn_kernel(x_ref, g_ref, wu_ref, wd_ref, o_ref, h_ref):
    @pl.when(pl.program_id(1) == 0)
    def _():
        xf = x_ref[...]
        h_ref[...] = _rms(xf, g_ref[...]).astype(BF16)
        o_ref[...] = xf

    up = jnp.dot(h_ref[...], wu_ref[...], preferred_element_type=F32)
    act = jnp.square(jnp.maximum(up, 0.0)).astype(BF16)
    o_ref[...] += jnp.dot(act, wd_ref[...], preferred_element_type=F32)


def _ffn(x2, g, wu, wd, tm=1024, tf=512):
    t = x2.shape[0]
    return pl.pallas_call(
        _ffn_kernel,
        grid=(t // tm, D_FF // tf),
        in_specs=[
            pl.BlockSpec((tm, D_MODEL), lambda i, f: (i, 0)),
            pl.BlockSpec((1, D_MODEL), lambda i, f: (0, 0)),
            pl.BlockSpec((D_MODEL, tf), lambda i, f: (0, f)),
            pl.BlockSpec((tf, D_MODEL), lambda i, f: (f, 0)),
        ],
        out_specs=pl.BlockSpec((tm, D_MODEL), lambda i, f: (i, 0)),
        out_shape=jax.ShapeDtypeStruct((t, D_MODEL), F32),
        scratch_shapes=[pltpu.VMEM((tm, D_MODEL), BF16)],
        compiler_params=_params("parallel", "arbitrary"),
        name="ffn",
    )(x2, g, wu, wd)


def _final_norm_kernel(x_ref, g_ref, o_ref):
    o_ref[...] = _rms(x_ref[...], g_ref[...])


def _final_norm(x2, g, tm=1024):
    t = x2.shape[0]
    return pl.pallas_call(
        _final_norm_kernel,
        grid=(t // tm,),
        in_specs=[pl.BlockSpec((tm, D_MODEL), lambda i: (i, 0)),
                  pl.BlockSpec((1, D_MODEL), lambda i: (0, 0))],
        out_specs=pl.BlockSpec((tm, D_MODEL), lambda i: (i, 0)),
        out_shape=jax.ShapeDtypeStruct((t, D_MODEL), F32),
        compiler_params=_params("parallel"),
        name="final_norm",
    )(x2, g)


def _pack_in_weights(w):
    q_end = _QKV_W
    u_end = q_end + _U_W
    cq_end = u_end + MLA_Q_RANK
    ckv_end = cq_end + MLA_KV_RANK
    kpe_end = ckv_end + MLA_ROPE_DIM
    half = MLA_ROPE_DIM // 2
    kpe = w[:, ckv_end:kpe_end]
    z_lo = jnp.zeros((w.shape[0], MLA_NOPE_DIM), w.dtype)
    z_hi = jnp.zeros((w.shape[0], LANES - MLA_NOPE_DIM - MLA_ROPE_DIM), w.dtype)
    kpe_a = jnp.concatenate([z_lo, kpe, z_hi], axis=1)
    kpe_b = jnp.concatenate([z_lo, kpe[:, half:], kpe[:, :half], z_hi], axis=1)
    return jnp.concatenate([w[:, :ckv_end], kpe_a, kpe_b, w[:, kpe_end:]], axis=1).astype(BF16)


def _pack_q_up(w):
    r = w.shape[0]
    half = MLA_ROPE_DIM // 2
    wh = w.reshape(r, MLA_HEADS, MLA_NOPE_DIM + MLA_ROPE_DIM)
    nope, pe = wh[..., :MLA_NOPE_DIM], wh[..., MLA_NOPE_DIM:]
    pad = jnp.zeros((r, MLA_HEADS, LANES - MLA_NOPE_DIM - MLA_ROPE_DIM), w.dtype)
    full = jnp.concatenate([nope, pe, pad], axis=-1)
    rot = jnp.concatenate([jnp.zeros_like(nope), pe[..., half:], pe[..., :half], pad], axis=-1)
    return (full.reshape(r, MLA_HEADS * LANES).astype(BF16),
            rot.reshape(r, MLA_HEADS * LANES).astype(BF16))


def _pack_kv_up(w):
    r = w.shape[0]
    wh = w.reshape(r, MLA_HEADS, MLA_NOPE_DIM + MLA_V_DIM)
    kn, v = wh[..., :MLA_NOPE_DIM], wh[..., MLA_NOPE_DIM:]
    k_full = jnp.concatenate([kn, jnp.zeros((r, MLA_HEADS, LANES - MLA_NOPE_DIM), w.dtype)], axis=-1)
    return (k_full.reshape(r, MLA_HEADS * LANES).astype(BF16),
            v.reshape(r, MLA_HEADS * MLA_V_DIM).astype(BF16))


def _rope_lane_patterns():
    half = MLA_ROPE_DIM // 2
    freqs = jnp.exp(-math.log(ROPE_THETA) * jnp.arange(half, dtype=F32) / half)
    zeros_lo = jnp.zeros((MLA_NOPE_DIM,), F32)
    zeros_hi = jnp.zeros((LANES - MLA_NOPE_DIM - MLA_ROPE_DIM,), F32)
    fpat = jnp.concatenate([zeros_lo, freqs, freqs, zeros_hi]).reshape(1, LANES)
    ones = jnp.ones((half,), F32)
    sgn = jnp.concatenate([zeros_lo, -ones, ones, zeros_hi]).reshape(1, LANES)
    return fpat, sgn


def kernel(x, positions, rel_bias, g_final, g_mix, w_in, swa_sinks, g_q_norm, w_q_up, g_kv_norm, w_kv_up, w_dw, b_dw, g_conv_ln, b_conv_ln, w_branch, w_out, g_mlp, w_up, w_down):
    b, s, d = x.shape
    t = b * s
    x2 = x.reshape(t, d)
    pos_col = positions.reshape(t, 1)
    fpat, sgn = _rope_lane_patterns()
    bias = _swa_bias(positions, rel_bias)
    for l in range(DEPTH):
        qkv, u, cmla, gates = _in_proj(x2, g_mix[l].reshape(1, d), _pack_in_weights(w_in[l]))
        y_a = _swa(qkv.reshape(b, s, _QKV_W), bias, swa_sinks[l])
        y_b = _conv(u.reshape(b, s, _U_W), w_dw[l].reshape(CONV_WIDTH, CONV_CHANNELS),
                    b_dw[l].reshape(1, -1), g_conv_ln[l].reshape(1, -1), b_conv_ln[l].reshape(1, -1))
        wq, wqr = _pack_q_up(w_q_up[l])
        wk, wv = _pack_kv_up(w_kv_up[l])
        qf, kf, v = _mla_prep(cmla, pos_col, fpat, sgn, g_q_norm[l].reshape(1, -1),
                              g_kv_norm[l].reshape(1, -1), wq, wqr, wk, wv)
        hw = MLA_HEADS * LANES
        y_c = _mla_attn(qf.reshape(b, s, hw), kf.reshape(b, s, hw),
                        v.reshape(b, s, MLA_HEADS * MLA_V_DIM))
        x2 = _merge(y_a.reshape(t, -1), y_b.reshape(t, -1), y_c.reshape(t, -1), gates, x2,
                    w_branch[l].astype(BF16), w_out[l].astype(BF16))
        x2 = _ffn(x2, g_mlp[l].reshape(1, d), w_up[l].astype(BF16), w_down[l].astype(BF16))
    return _final_norm(x2, g_final.reshape(1, d)).reshape(b, s, d)
```

```python
import functools
import math

import jax
import jax.numpy as jnp
from jax import lax
from jax.experimental import pallas as pl
from jax.experimental.pallas import tpu as pltpu

D_MODEL = 1024
DEPTH = 4
BRANCH_WIDTH = 512
N_BRANCHES = 3
SWA_Q_HEADS = 8
SWA_KV_HEADS = 2
SWA_GROUP = SWA_Q_HEADS // SWA_KV_HEADS
SWA_HEAD_DIM = 64
WINDOW = 128
N_BUCKETS = 32
MAX_DISTANCE = 128
CONV_CHANNELS = 512
CONV_WIDTH = 31
MLA_HEADS = 8
MLA_Q_RANK = 256
MLA_KV_RANK = 128
MLA_NOPE_DIM = 64
MLA_ROPE_DIM = 32
MLA_V_DIM = 64
ROPE_THETA = 10000.0
D_FF = 4 * D_MODEL
EPS = 1e-6
NEG_INF = -1e30

LANES = 128
SUBLANES = 8
VMEM_LIMIT = 56 * 1024 * 1024

F32 = jnp.float32
BF16 = jnp.bfloat16

_MAX_EXACT = N_BUCKETS // 2
_T5_THRESHOLDS = tuple(range(1, _MAX_EXACT + 1)) + tuple(
    math.ceil(_MAX_EXACT * (MAX_DISTANCE / _MAX_EXACT) ** (j / (N_BUCKETS - _MAX_EXACT)))
    for j in range(1, N_BUCKETS - _MAX_EXACT))

_QKV_W = 768
_U_W = 2 * CONV_CHANNELS
_CMLA_W = MLA_Q_RANK + MLA_KV_RANK + 2 * LANES
_GATE_W = N_BRANCHES * D_MODEL
_IN_W = _QKV_W + _U_W + _CMLA_W + _GATE_W


def _params(*semantics):
    return pltpu.CompilerParams(dimension_semantics=semantics, vmem_limit_bytes=VMEM_LIMIT)


def _rms(xf, g):
    return xf * lax.rsqrt(jnp.mean(xf * xf, axis=-1, keepdims=True) + EPS) * g


def _bias_kernel(posq_ref, pkp_ref, pkc_ref, tbl_ref, out_ref):
    blk = pl.program_id(1)
    pk_prev = jnp.where(blk > 0, pkp_ref[0], 0)
    pk = jnp.concatenate([pk_prev, pkc_ref[0]], axis=1)
    ki = lax.broadcasted_iota(jnp.int32, (SUBLANES, 2 * WINDOW), 1)
    has_prev = (blk > 0) | (ki >= WINDOW)

    def rows(r, carry):
        r0 = pl.multiple_of(r * SUBLANES, SUBLANES)
        pq = posq_ref[0, pl.ds(r0, SUBLANES), :]
        n = jnp.maximum(pq - pk, 0)
        qi = WINDOW + r0 + lax.broadcasted_iota(jnp.int32, (SUBLANES, 2 * WINDOW), 0)
        mask = (ki <= qi) & (qi - ki < WINDOW) & has_prev
        vals = [jnp.broadcast_to(tbl_ref[pl.ds(h, 1), :], n.shape) for h in range(SWA_Q_HEADS)]
        for b in range(1, N_BUCKETS):
            ge = n >= _T5_THRESHOLDS[b - 1]
            for h in range(SWA_Q_HEADS):
                t = jnp.broadcast_to(tbl_ref[pl.ds(b * SWA_Q_HEADS + h, 1), :], n.shape)
                vals[h] = jnp.where(ge, t, vals[h])
        for h in range(SWA_Q_HEADS):
            out_ref[0, 0, h, pl.ds(r0, SUBLANES), :] = jnp.where(mask, vals[h], NEG_INF)
        return carry

    lax.fori_loop(0, WINDOW // SUBLANES, rows, 0)


def _swa_bias(positions, rel_bias):
    b, s = positions.shape
    nb = s // WINDOW
    tbl = jnp.broadcast_to(rel_bias.astype(F32).reshape(N_BUCKETS * SWA_Q_HEADS, 1),
                           (N_BUCKETS * SWA_Q_HEADS, 2 * WINDOW))
    pos_col = positions.reshape(b, s, 1)
    pos_row = positions.reshape(b, 1, s)
    return pl.pallas_call(
        _bias_kernel,
        grid=(b, nb),
        in_specs=[
            pl.BlockSpec((1, WINDOW, 1), lambda i, n: (i, n, 0)),
            pl.BlockSpec((1, 1, WINDOW), lambda i, n: (i, 0, jnp.maximum(n - 1, 0))),
            pl.BlockSpec((1, 1, WINDOW), lambda i, n: (i, 0, n)),
            pl.BlockSpec((N_BUCKETS * SWA_Q_HEADS, 2 * WINDOW), lambda i, n: (0, 0)),
        ],
        out_specs=pl.BlockSpec((1, 1, SWA_Q_HEADS, WINDOW, 2 * WINDOW), lambda i, n: (i, n, 0, 0, 0)),
        out_shape=jax.ShapeDtypeStruct((b, nb, SWA_Q_HEADS, WINDOW, 2 * WINDOW), F32),
        compiler_params=_params("parallel", "parallel"),
        name="swa_bias",
    )(pos_col, pos_row, pos_row, tbl)


def _in_proj_kernel(x_ref, g_ref, w_ref, qkv_ref, u_ref, c_ref, gate_ref, *, chunk):
    h = _rms(x_ref[...], g_ref[...]).astype(BF16)
    col = 0
    for out_ref in (qkv_ref, u_ref, c_ref, gate_ref):
        width = out_ref.shape[1]
        for c0 in range(0, width, chunk):
            c1 = min(c0 + chunk, width)
            out_ref[:, c0:c1] = jnp.dot(h, w_ref[:, col + c0:col + c1],
                                        preferred_element_type=F32).astype(out_ref.dtype)
        col += width


def _in_proj(x2, g, w, tm=512):
    t = x2.shape[0]
    row = lambda i: (i, 0)
    const = lambda i: (0, 0)
    return pl.pallas_call(
        functools.partial(_in_proj_kernel, chunk=512),
        grid=(t // tm,),
        in_specs=[
            pl.BlockSpec((tm, D_MODEL), row),
            pl.BlockSpec((1, D_MODEL), const),
            pl.BlockSpec((D_MODEL, _IN_W), const),
        ],
        out_specs=[
            pl.BlockSpec((tm, _QKV_W), row),
            pl.BlockSpec((tm, _U_W), row),
            pl.BlockSpec((tm, _CMLA_W), row),
            pl.BlockSpec((tm, _GATE_W), row),
        ],
        out_shape=[
            jax.ShapeDtypeStruct((t, _QKV_W), BF16),
            jax.ShapeDtypeStruct((t, _U_W), F32),
            jax.ShapeDtypeStruct((t, _CMLA_W), F32),
            jax.ShapeDtypeStruct((t, _GATE_W), F32),
        ],
        compiler_params=_params("parallel"),
        name="in_proj",
    )(x2, g, w)


def _swa_kernel(q_ref, kp_ref, kc_ref, vp_ref, vc_ref, bias_ref, sink_ref, o_ref):
    k = jnp.concatenate([kp_ref[0], kc_ref[0]], axis=0)
    v = jnp.concatenate([vp_ref[0], vc_ref[0]], axis=0)
    scale = SWA_HEAD_DIM ** -0.5
    outs = []
    for h in range(SWA_Q_HEADS):
        hk = h // SWA_GROUP
        qh = q_ref[0, :, h * SWA_HEAD_DIM:(h + 1) * SWA_HEAD_DIM]
        kh = k[:, hk * SWA_HEAD_DIM:(hk + 1) * SWA_HEAD_DIM]
        vh = v[:, hk * SWA_HEAD_DIM:(hk + 1) * SWA_HEAD_DIM]
        s = lax.dot_general(qh, kh, (((1,), (1,)), ((), ())), preferred_element_type=F32)
        s = s * scale + bias_ref[0, 0, h]
        sink = sink_ref[h]
        m = jnp.maximum(jnp.max(s, axis=1, keepdims=True), sink)
        p = jnp.exp(s - m)
        denom = jnp.sum(p, axis=1, keepdims=True) + jnp.exp(sink - m)
        o = jnp.dot(p.astype(BF16), vh, preferred_element_type=F32)
        outs.append(o / denom)
    o_ref[0] = jnp.concatenate(outs, axis=1).astype(o_ref.dtype)


def _swa(qkv, bias, sinks):
    b, s, _ = qkv.shape
    nb = s // WINDOW
    kcol = (SWA_Q_HEADS * SWA_HEAD_DIM) // LANES
    vcol = kcol + 1
    prev = lambda n: jnp.maximum(n - 1, 0)
    return pl.pallas_call(
        _swa_kernel,
        grid=(b, nb),
        in_specs=[
            pl.BlockSpec((1, WINDOW, SWA_Q_HEADS * SWA_HEAD_DIM), lambda i, n: (i, n, 0)),
            pl.BlockSpec((1, WINDOW, LANES), lambda i, n: (i, prev(n), kcol)),
            pl.BlockSpec((1, WINDOW, LANES), lambda i, n: (i, n, kcol)),
            pl.BlockSpec((1, WINDOW, LANES), lambda i, n: (i, prev(n), vcol)),
            pl.BlockSpec((1, WINDOW, LANES), lambda i, n: (i, n, vcol)),
            pl.BlockSpec((1, 1, SWA_Q_HEADS, WINDOW, 2 * WINDOW), lambda i, n: (i, n, 0, 0, 0)),
            pl.BlockSpec(memory_space=pltpu.SMEM),
        ],
        out_specs=pl.BlockSpec((1, WINDOW, BRANCH_WIDTH), lambda i, n: (i, n, 0)),
        out_shape=jax.ShapeDtypeStruct((b, s, BRANCH_WIDTH), BF16),
        compiler_params=_params("parallel", "parallel"),
        name="swa_attn",
    )(qkv, qkv, qkv, qkv, qkv, bias, sinks)


_CONV_HALO = 32
_CONV_CHUNK = 32


def _conv_kernel(uh_ref, uc_ref, w_ref, bdw_ref, g_ref, b_ref, o_ref, ext_ref, *, ts):
    c = CONV_CHANNELS

    def glu(u):
        return u[:, :c] * jax.nn.sigmoid(u[:, c:])

    halo = glu(uh_ref[0])
    ext_ref[0:_CONV_HALO, :] = jnp.where(pl.program_id(1) > 0, halo, 0.0)
    ext_ref[_CONV_HALO:, :] = glu(uc_ref[0])
    lead = _CONV_HALO - (CONV_WIDTH - 1)
    for r0 in range(0, ts, _CONV_CHUNK):
        acc = jnp.broadcast_to(bdw_ref[...], (_CONV_CHUNK, c))
        for j in range(CONV_WIDTH):
            start = r0 + lead + j
            acc = acc + w_ref[j:j + 1, :] * ext_ref[start:start + _CONV_CHUNK, :]
        mu = jnp.mean(acc, axis=-1, keepdims=True)
        d = acc - mu
        var = jnp.mean(d * d, axis=-1, keepdims=True)
        y = d * lax.rsqrt(var + EPS) * g_ref[...] + b_ref[...]
        o_ref[0, r0:r0 + _CONV_CHUNK, :] = (y * jax.nn.sigmoid(y)).astype(o_ref.dtype)


def _conv(u, w_dw, b_dw, g_ln, b_ln, ts=256):
    b, s, _ = u.shape
    ratio = ts // _CONV_HALO
    const = lambda i, n: (0, 0)
    return pl.pallas_call(
        functools.partial(_conv_kernel, ts=ts),
        grid=(b, s // ts),
        in_specs=[
            pl.BlockSpec((1, _CONV_HALO, _U_W), lambda i, n: (i, jnp.maximum(n * ratio - 1, 0), 0)),
            pl.BlockSpec((1, ts, _U_W), lambda i, n: (i, n, 0)),
            pl.BlockSpec((CONV_WIDTH, CONV_CHANNELS), const),
            pl.BlockSpec((1, CONV_CHANNELS), const),
            pl.BlockSpec((1, CONV_CHANNELS), const),
            pl.BlockSpec((1, CONV_CHANNELS), const),
        ],
        out_specs=pl.BlockSpec((1, ts, CONV_CHANNELS), lambda i, n: (i, n, 0)),
        out_shape=jax.ShapeDtypeStruct((b, s, CONV_CHANNELS), BF16),
        scratch_shapes=[pltpu.VMEM((ts + _CONV_HALO, CONV_CHANNELS), F32)],
        compiler_params=_params("parallel", "parallel"),
        name="conformer_conv",
    )(u, u, w_dw, b_dw, g_ln, b_ln)


def _mla_prep_kernel(c_ref, pos_ref, fpat_ref, sgn_ref, gq_ref, gkv_ref,
                     wq_ref, wqr_ref, wk_ref, wv_ref, q_ref, k_ref, v_ref):
    cq = c_ref[:, 0:MLA_Q_RANK]
    ckv = c_ref[:, MLA_Q_RANK:MLA_Q_RANK + MLA_KV_RANK]
    kpe_a = c_ref[:, MLA_Q_RANK + MLA_KV_RANK:MLA_Q_RANK + MLA_KV_RANK + LANES]
    kpe_b = c_ref[:, MLA_Q_RANK + MLA_KV_RANK + LANES:]
    ang = pos_ref[...].astype(F32) * fpat_ref[...]
    sgn = sgn_ref[...]
    cos = jnp.where(sgn != 0.0, jnp.cos(ang), 1.0)
    sin = jnp.sin(ang) * sgn
    hq = _rms(cq, gq_ref[...]).astype(BF16)
    hkv = _rms(ckv, gkv_ref[...]).astype(BF16)
    k_rope = kpe_a * cos + kpe_b * sin
    q_scale = (MLA_NOPE_DIM + MLA_ROPE_DIM) ** -0.5 * math.log2(math.e)
    cos_q = cos * q_scale
    sin_q = sin * q_scale
    for h in range(MLA_HEADS):
        sl = slice(h * LANES, (h + 1) * LANES)
        qf = jnp.dot(hq, wq_ref[:, sl], preferred_element_type=F32)
        qr = jnp.dot(hq, wqr_ref[:, sl], preferred_element_type=F32)
        q_ref[:, sl] = (qf * cos_q + qr * sin_q).astype(q_ref.dtype)
        kn = jnp.dot(hkv, wk_ref[:, sl], preferred_element_type=F32)
        k_ref[:, sl] = (kn + k_rope).astype(k_ref.dtype)
    half_block = lax.broadcasted_iota(jnp.int32, (1, v_ref.shape[1]), 1) // MLA_V_DIM
    ones = ((half_block % 4 == 1) | (half_block % 4 == 2)).astype(F32)
    v_ref[...] = (jnp.dot(hkv, wv_ref[...], preferred_element_type=F32) + ones).astype(v_ref.dtype)


def _mla_prep(cmla, pos_col, fpat, sgn, gq, gkv, wq, wqr, wk, wv, ts=512):
    t = cmla.shape[0]
    row = lambda i: (i, 0)
    const = lambda i: (0, 0)
    hw = MLA_HEADS * LANES
    return pl.pallas_call(
        _mla_prep_kernel,
        grid=(t // ts,),
        in_specs=[
            pl.BlockSpec((ts, _CMLA_W), row),
            pl.BlockSpec((ts, 1), row),
            pl.BlockSpec((1, LANES), const),
            pl.BlockSpec((1, LANES), const),
            pl.BlockSpec((1, MLA_Q_RANK), const),
            pl.BlockSpec((1, MLA_KV_RANK), const),
            pl.BlockSpec((MLA_Q_RANK, hw), const),
            pl.BlockSpec((MLA_Q_RANK, hw), const),
            pl.BlockSpec((MLA_KV_RANK, hw), const),
            pl.BlockSpec((MLA_KV_RANK, hw), const),
        ],
        out_specs=[
            pl.BlockSpec((ts, hw), row),
            pl.BlockSpec((ts, hw), row),
            pl.BlockSpec((ts, hw), row),
        ],
        out_shape=[
            jax.ShapeDtypeStruct((t, hw), BF16),
            jax.ShapeDtypeStruct((t, hw), BF16),
            jax.ShapeDtypeStruct((t, hw), BF16),
        ],
        compiler_params=_params("parallel"),
        name="mla_prep",
    )(cmla, pos_col, fpat, sgn, gq, gkv, wq, wqr, wk, wv)


def _mla_attn_kernel(q_ref, k_ref, v_ref, o_ref, m_ref, acc_ref, *, tq, tk):
    i = pl.program_id(2)
    n_diag = tq // tk
    m_ref[...] = jnp.full(m_ref.shape, NEG_INF, F32)
    acc_ref[...] = jnp.zeros(acc_ref.shape, F32)

    def step(j, masked):
        k0 = pl.multiple_of(j * tk, tk)
        for hh in range(2):
            lanes = slice(hh * LANES, (hh + 1) * LANES)
            q = q_ref[0, :, lanes]
            k = k_ref[0, pl.ds(k0, tk), lanes]
            v = v_ref[0, pl.ds(k0, tk), lanes]
            s = lax.dot_general(q, k, (((1,), (1,)), ((), ())), preferred_element_type=F32)
            if masked:
                row = lax.broadcasted_iota(jnp.int32, (tq, tk), 0)
                col = lax.broadcasted_iota(jnp.int32, (tq, tk), 1)
                s = jnp.where(col + (j * tk - i * tq) <= row, s, NEG_INF)
            m_old = m_ref[hh]
            m_new = jnp.maximum(m_old, jnp.max(s, axis=1, keepdims=True))
            alpha = jnp.exp2(m_old - m_new)
            p = jnp.exp2(s - jnp.concatenate([m_new] * (tk // LANES), axis=1))
            m_ref[hh] = m_new
            acc_ref[hh] = alpha * acc_ref[hh] + jnp.dot(p.astype(BF16), v, preferred_element_type=F32)

    def full_tile(j, carry):
        step(j, masked=False)
        return carry

    lax.fori_loop(0, i * n_diag, full_tile, 0)
    for d in range(n_diag):
        step(i * n_diag + d, masked=True)
    lane = lax.broadcasted_iota(jnp.int32, (tq, LANES), 1)
    acc = jnp.where(lane < MLA_V_DIM, acc_ref[0], acc_ref[1])
    den = jnp.where(lane < MLA_V_DIM, pltpu.roll(acc_ref[0], MLA_V_DIM, 1),
                    pltpu.roll(acc_ref[1], MLA_V_DIM, 1))
    o_ref[0] = (acc / den).astype(o_ref.dtype)


def _mla_attn(qf, kf, v, tq=512, tk=512):
    b, s, _ = qf.shape
    pairs = MLA_HEADS // 2
    return pl.pallas_call(
        functools.partial(_mla_attn_kernel, tq=tq, tk=tk),
        grid=(b, pairs, s // tq),
        in_specs=[
            pl.BlockSpec((1, tq, 2 * LANES), lambda i, p, n: (i, n, p)),
            pl.BlockSpec((1, s, 2 * LANES), lambda i, p, n: (i, 0, p)),
            pl.BlockSpec((1, s, 2 * LANES), lambda i, p, n: (i, 0, p)),
        ],
        out_specs=pl.BlockSpec((1, tq, LANES), lambda i, p, n: (i, n, p)),
        out_shape=jax.ShapeDtypeStruct((b, s, MLA_HEADS * MLA_V_DIM), BF16),
        scratch_shapes=[pltpu.VMEM((2, tq, LANES), F32), pltpu.VMEM((2, tq, LANES), F32)],
        compiler_params=_params("parallel", "parallel", "arbitrary"),
        name="mla_attn",
    )(qf, kf, v)


def _merge_kernel(ya_ref, yb_ref, yc_ref, gate_ref, x_ref, wb_ref, wo_ref, o_ref):
    merged = None
    for n, y_ref in enumerate((ya_ref, yb_ref, yc_ref)):
        br = jnp.dot(y_ref[...], wb_ref[n], preferred_element_type=F32)
        term = jax.nn.sigmoid(gate_ref[:, n * D_MODEL:(n + 1) * D_MODEL]) * br
        merged = term if merged is None else merged + term
    o_ref[...] = x_ref[...] + jnp.dot(merged.astype(BF16), wo_ref[...], preferred_element_type=F32)


def _merge(ya, yb, yc, gates, x2, wb, wo, tm=512):
    t = x2.shape[0]
    row = lambda i: (i, 0)
    return pl.pallas_call(
        _merge_kernel,
        grid=(t // tm,),
        in_specs=[
            pl.BlockSpec((tm, BRANCH_WIDTH), row),
            pl.BlockSpec((tm, BRANCH_WIDTH), row),
            pl.BlockSpec((tm, BRANCH_WIDTH), row),
            pl.BlockSpec((tm, _GATE_W), row),
            pl.BlockSpec((tm, D_MODEL), row),
            pl.BlockSpec((N_BRANCHES, BRANCH_WIDTH, D_MODEL), lambda i: (0, 0, 0)),
            pl.BlockSpec((D_MODEL, D_MODEL), lambda i: (0, 0)),
        ],
        out_specs=pl.BlockSpec((tm, D_MODEL), row),
        out_shape=jax.ShapeDtypeStruct((t, D_MODEL), F32),
        compiler_params=_params("parallel"),
        name="merge_out",
    )(ya, yb, yc, gates, x2, wb, wo)


def _ffn_kernel(x_ref, g_ref, wu_ref, wd_ref, o_ref, h_ref):
    @pl.when(pl.program_id(1) == 0)
    def _():
        xf = x_ref[...]
        h_ref[...] = _rms(xf, g_ref[...]).astype(BF16)
        o_ref[...] = xf

    up = jnp.dot(h_ref[...], wu_ref[...], preferred_element_type=F32)
    act = jnp.square(jnp.maximum(up, 0.0)).astype(BF16)
    o_ref[...] += jnp.dot(act, wd_ref[...], preferred_element_type=F32)


def _ffn(x2, g, wu, wd, tm=1024, tf=512):
    t = x2.shape[0]
    return pl.pallas_call(
        _ffn_kernel,
        grid=(t // tm, D_FF // tf),
        in_specs=[
            pl.BlockSpec((tm, D_MODEL), lambda i, f: (i, 0)),
            pl.BlockSpec((1, D_MODEL), lambda i, f: (0, 0)),
            pl.BlockSpec((D_MODEL, tf), lambda i, f: (0, f)),
            pl.BlockSpec((tf, D_MODEL), lambda i, f: (f, 0)),
        ],
        out_specs=pl.BlockSpec((tm, D_MODEL), lambda i, f: (i, 0)),
        out_shape=jax.ShapeDtypeStruct((t, D_MODEL), F32),
        scratch_shapes=[pltpu.VMEM((tm, D_MODEL), BF16)],
        compiler_params=_params("parallel", "arbitrary"),
        name="ffn",
    )(x2, g, wu, wd)


def _final_norm_kernel(x_ref, g_ref, o_ref):
    o_ref[...] = _rms(x_ref[...], g_ref[...])


def _final_norm(x2, g, tm=1024):
    t = x2.shape[0]
    return pl.pallas_call(
        _final_norm_kernel,
        grid=(t // tm,),
        in_specs=[pl.BlockSpec((tm, D_MODEL), lambda i: (i, 0)),
                  pl.BlockSpec((1, D_MODEL), lambda i: (0, 0))],
        out_specs=pl.BlockSpec((tm, D_MODEL), lambda i: (i, 0)),
        out_shape=jax.ShapeDtypeStruct((t, D_MODEL), F32),
        compiler_params=_params("parallel"),
        name="final_norm",
    )(x2, g)


def _pack_in_weights(w):
    q_end = _QKV_W
    u_end = q_end + _U_W
    cq_end = u_end + MLA_Q_RANK
    ckv_end = cq_end + MLA_KV_RANK
    kpe_end = ckv_end + MLA_ROPE_DIM
    half = MLA_ROPE_DIM // 2
    kpe = w[:, ckv_end:kpe_end]
    z_lo = jnp.zeros((w.shape[0], MLA_NOPE_DIM), w.dtype)
    z_hi = jnp.zeros((w.shape[0], LANES - MLA_NOPE_DIM - MLA_ROPE_DIM), w.dtype)
    kpe_a = jnp.concatenate([z_lo, kpe, z_hi], axis=1)
    kpe_b = jnp.concatenate([z_lo, kpe[:, half:], kpe[:, :half], z_hi], axis=1)
    return jnp.concatenate([w[:, :ckv_end], kpe_a, kpe_b, w[:, kpe_end:]], axis=1).astype(BF16)


def _pack_q_up(w):
    r = w.shape[0]
    half = MLA_ROPE_DIM // 2
    wh = w.reshape(r, MLA_HEADS, MLA_NOPE_DIM + MLA_ROPE_DIM)
    nope, pe = wh[..., :MLA_NOPE_DIM], wh[..., MLA_NOPE_DIM:]
    pad = jnp.zeros((r, MLA_HEADS, LANES - MLA_NOPE_DIM - MLA_ROPE_DIM), w.dtype)
    full = jnp.concatenate([nope, pe, pad], axis=-1)
    rot = jnp.concatenate([jnp.zeros_like(nope), pe[..., half:], pe[..., :half], pad], axis=-1)
    return (full.reshape(r, MLA_HEADS * LANES).astype(BF16),
            rot.reshape(r, MLA_HEADS * LANES).astype(BF16))


def _pack_kv_up(w):
    r = w.shape[0]
    wh = w.reshape(r, MLA_HEADS // 2, 2, MLA_NOPE_DIM + MLA_V_DIM)
    kn, v = wh[..., :MLA_NOPE_DIM], wh[..., MLA_NOPE_DIM:]
    zk = jnp.zeros_like(kn)
    k_full = jnp.concatenate([kn, zk], axis=-1)
    zv = jnp.zeros_like(v[:, :, 0])
    v_full = jnp.concatenate([v[:, :, 0], zv, zv, v[:, :, 1]], axis=-1)
    return (k_full.reshape(r, MLA_HEADS * LANES).astype(BF16),
            v_full.reshape(r, MLA_HEADS * LANES).astype(BF16))


def _rope_lane_patterns():
    half = MLA_ROPE_DIM // 2
    freqs = jnp.exp(-math.log(ROPE_THETA) * jnp.arange(half, dtype=F32) / half)
    zeros_lo = jnp.zeros((MLA_NOPE_DIM,), F32)
    zeros_hi = jnp.zeros((LANES - MLA_NOPE_DIM - MLA_ROPE_DIM,), F32)
    fpat = jnp.concatenate([zeros_lo, freqs, freqs, zeros_hi]).reshape(1, LANES)
    ones = jnp.ones((half,), F32)
    sgn = jnp.concatenate([zeros_lo, -ones, ones, zeros_hi]).reshape(1, LANES)
    return fpat, sgn


def kernel(x, positions, rel_bias, g_final, g_mix, w_in, swa_sinks, g_q_norm, w_q_up, g_kv_norm, w_kv_up, w_dw, b_dw, g_conv_ln, b_conv_ln, w_branch, w_out, g_mlp, w_up, w_down):
    b, s, d = x.shape
    t = b * s
    x2 = x.reshape(t, d)
    pos_col = positions.reshape(t, 1)
    fpat, sgn = _rope_lane_patterns()
    bias = _swa_bias(positions, rel_bias)
    for l in range(DEPTH):
        qkv, u, cmla, gates = _in_proj(x2, g_mix[l].reshape(1, d), _pack_in_weights(w_in[l]))
        y_a = _swa(qkv.reshape(b, s, _QKV_W), bias, swa_sinks[l])
        y_b = _conv(u.reshape(b, s, _U_W), w_dw[l].reshape(CONV_WIDTH, CONV_CHANNELS),
                    b_dw[l].reshape(1, -1), g_conv_ln[l].reshape(1, -1), b_conv_ln[l].reshape(1, -1))
        wq, wqr = _pack_q_up(w_q_up[l])
        wk, wv = _pack_kv_up(w_kv_up[l])
        qf, kf, v = _mla_prep(cmla, pos_col, fpat, sgn, g_q_norm[l].reshape(1, -1),
                              g_kv_norm[l].reshape(1, -1), wq, wqr, wk, wv)
        hw = MLA_HEADS * LANES
        y_c = _mla_attn(qf.reshape(b, s, hw), kf.reshape(b, s, hw),
                        v.reshape(b, s, hw))
        x2 = _merge(y_a.reshape(t, -1), y_b.reshape(t, -1), y_c.reshape(t, -1), gates, x2,
                    w_branch[l].astype(BF16), w_out[l].astype(BF16))
        x2 = _ffn(x2, g_mlp[l].reshape(1, d), w_up[l].astype(BF16), w_down[l].astype(BF16))
    return _final_norm(x2, g_final.reshape(1, d)).reshape(b, s, d)
```

```python
import functools
import math

import jax
import jax.numpy as jnp
from jax import lax
from jax.experimental import pallas as pl
from jax.experimental.pallas import tpu as pltpu

D_MODEL = 1024
DEPTH = 4
BRANCH_WIDTH = 512
N_BRANCHES = 3
SWA_Q_HEADS = 8
SWA_KV_HEADS = 2
SWA_GROUP = SWA_Q_HEADS // SWA_KV_HEADS
SWA_HEAD_DIM = 64
WINDOW = 128
N_BUCKETS = 32
MAX_DISTANCE = 128
CONV_CHANNELS = 512
CONV_WIDTH = 31
MLA_HEADS = 8
MLA_Q_RANK = 256
MLA_KV_RANK = 128
MLA_NOPE_DIM = 64
MLA_ROPE_DIM = 32
MLA_V_DIM = 64
ROPE_THETA = 10000.0
D_FF = 4 * D_MODEL
EPS = 1e-6
NEG_INF = -1e30

LANES = 128
SUBLANES = 8
VMEM_LIMIT = 56 * 1024 * 1024

F32 = jnp.float32
BF16 = jnp.bfloat16

_MAX_EXACT = N_BUCKETS // 2
_T5_THRESHOLDS = tuple(range(1, _MAX_EXACT + 1)) + tuple(
    math.ceil(_MAX_EXACT * (MAX_DISTANCE / _MAX_EXACT) ** (j / (N_BUCKETS - _MAX_EXACT)))
    for j in range(1, N_BUCKETS - _MAX_EXACT))

_QKV_W = 768
_U_W = 2 * CONV_CHANNELS
_CMLA_W = MLA_Q_RANK + MLA_KV_RANK + 2 * LANES
_GATE_W = N_BRANCHES * D_MODEL
_IN_W = _QKV_W + _U_W + _CMLA_W + _GATE_W


def _params(*semantics):
    return pltpu.CompilerParams(dimension_semantics=semantics, vmem_limit_bytes=VMEM_LIMIT)


def _rms(xf, g):
    return xf * lax.rsqrt(jnp.mean(xf * xf, axis=-1, keepdims=True) + EPS) * g


def _bias_kernel(posq_ref, pkp_ref, pkc_ref, tbl_ref, out_ref):
    blk = pl.program_id(1)
    pk_prev = jnp.where(blk > 0, pkp_ref[0], 0)
    pk = jnp.concatenate([pk_prev, pkc_ref[0]], axis=1)
    ki = lax.broadcasted_iota(jnp.int32, (SUBLANES, 2 * WINDOW), 1)
    has_prev = (blk > 0) | (ki >= WINDOW)

    def rows(r, carry):
        r0 = pl.multiple_of(r * SUBLANES, SUBLANES)
        pq = posq_ref[0, pl.ds(r0, SUBLANES), :]
        n = jnp.maximum(pq - pk, 0)
        qi = WINDOW + r0 + lax.broadcasted_iota(jnp.int32, (SUBLANES, 2 * WINDOW), 0)
        mask = (ki <= qi) & (qi - ki < WINDOW) & has_prev
        vals = [jnp.broadcast_to(tbl_ref[pl.ds(h, 1), :], n.shape) for h in range(SWA_Q_HEADS)]
        for b in range(1, N_BUCKETS):
            ge = n >= _T5_THRESHOLDS[b - 1]
            for h in range(SWA_Q_HEADS):
                t = jnp.broadcast_to(tbl_ref[pl.ds(b * SWA_Q_HEADS + h, 1), :], n.shape)
                vals[h] = jnp.where(ge, t, vals[h])
        for h in range(SWA_Q_HEADS):
            out_ref[0, 0, h, pl.ds(r0, SUBLANES), :] = jnp.where(mask, vals[h], NEG_INF)
        return carry

    lax.fori_loop(0, WINDOW // SUBLANES, rows, 0)


def _swa_bias(positions, rel_bias):
    b, s = positions.shape
    nb = s // WINDOW
    tbl = jnp.broadcast_to(rel_bias.astype(F32).reshape(N_BUCKETS * SWA_Q_HEADS, 1),
                           (N_BUCKETS * SWA_Q_HEADS, 2 * WINDOW))
    pos_col = positions.reshape(b, s, 1)
    pos_row = positions.reshape(b, 1, s)
    return pl.pallas_call(
        _bias_kernel,
        grid=(b, nb),
        in_specs=[
            pl.BlockSpec((1, WINDOW, 1), lambda i, n: (i, n, 0)),
            pl.BlockSpec((1, 1, WINDOW), lambda i, n: (i, 0, jnp.maximum(n - 1, 0))),
            pl.BlockSpec((1, 1, WINDOW), lambda i, n: (i, 0, n)),
            pl.BlockSpec((N_BUCKETS * SWA_Q_HEADS, 2 * WINDOW), lambda i, n: (0, 0)),
        ],
        out_specs=pl.BlockSpec((1, 1, SWA_Q_HEADS, WINDOW, 2 * WINDOW), lambda i, n: (i, n, 0, 0, 0)),
        out_shape=jax.ShapeDtypeStruct((b, nb, SWA_Q_HEADS, WINDOW, 2 * WINDOW), F32),
        compiler_params=_params("parallel", "parallel"),
        name="swa_bias",
    )(pos_col, pos_row, pos_row, tbl)


def _in_proj_kernel(x_ref, g_ref, w_ref, qkv_ref, u_ref, c_ref, gate_ref, *, chunk):
    h = _rms(x_ref[...], g_ref[...]).astype(BF16)
    col = 0
    for out_ref in (qkv_ref, u_ref, c_ref, gate_ref):
        width = out_ref.shape[1]
        for c0 in range(0, width, chunk):
            c1 = min(c0 + chunk, width)
            out_ref[:, c0:c1] = jnp.dot(h, w_ref[:, col + c0:col + c1],
                                        preferred_element_type=F32).astype(out_ref.dtype)
        col += width


def _in_proj(x2, g, w, tm=512):
    t = x2.shape[0]
    row = lambda i: (i, 0)
    const = lambda i: (0, 0)
    return pl.pallas_call(
        functools.partial(_in_proj_kernel, chunk=512),
        grid=(t // tm,),
        in_specs=[
            pl.BlockSpec((tm, D_MODEL), row),
            pl.BlockSpec((1, D_MODEL), const),
            pl.BlockSpec((D_MODEL, _IN_W), const),
        ],
        out_specs=[
            pl.BlockSpec((tm, _QKV_W), row),
            pl.BlockSpec((tm, _U_W), row),
            pl.BlockSpec((tm, _CMLA_W), row),
            pl.BlockSpec((tm, _GATE_W), row),
        ],
        out_shape=[
            jax.ShapeDtypeStruct((t, _QKV_W), BF16),
            jax.ShapeDtypeStruct((t, _U_W), F32),
            jax.ShapeDtypeStruct((t, _CMLA_W), F32),
            jax.ShapeDtypeStruct((t, _GATE_W), F32),
        ],
        compiler_params=_params("parallel"),
        name="in_proj",
    )(x2, g, w)


def _swa_kernel(q_ref, kp_ref, kc_ref, vp_ref, vc_ref, bias_ref, sink_ref, o_ref):
    k = jnp.concatenate([kp_ref[0], kc_ref[0]], axis=0)
    v = jnp.concatenate([vp_ref[0], vc_ref[0]], axis=0)
    scale = SWA_HEAD_DIM ** -0.5
    outs = []
    for h in range(SWA_Q_HEADS):
        hk = h // SWA_GROUP
        qh = q_ref[0, :, h * SWA_HEAD_DIM:(h + 1) * SWA_HEAD_DIM]
        kh = k[:, hk * SWA_HEAD_DIM:(hk + 1) * SWA_HEAD_DIM]
        vh = v[:, hk * SWA_HEAD_DIM:(hk + 1) * SWA_HEAD_DIM]
        s = lax.dot_general(qh, kh, (((1,), (1,)), ((), ())), preferred_element_type=F32)
        s = s * scale + bias_ref[0, 0, h]
        sink = sink_ref[h]
        m = jnp.maximum(jnp.max(s, axis=1, keepdims=True), sink)
        p = jnp.exp(s - m)
        denom = jnp.sum(p, axis=1, keepdims=True) + jnp.exp(sink - m)
        o = jnp.dot(p.astype(BF16), vh, preferred_element_type=F32)
        outs.append(o / denom)
    o_ref[0] = jnp.concatenate(outs, axis=1).astype(o_ref.dtype)


def _swa(qkv, bias, sinks):
    b, s, _ = qkv.shape
    nb = s // WINDOW
    kcol = (SWA_Q_HEADS * SWA_HEAD_DIM) // LANES
    vcol = kcol + 1
    prev = lambda n: jnp.maximum(n - 1, 0)
    return pl.pallas_call(
        _swa_kernel,
        grid=(b, nb),
        in_specs=[
            pl.BlockSpec((1, WINDOW, SWA_Q_HEADS * SWA_HEAD_DIM), lambda i, n: (i, n, 0)),
            pl.BlockSpec((1, WINDOW, LANES), lambda i, n: (i, prev(n), kcol)),
            pl.BlockSpec((1, WINDOW, LANES), lambda i, n: (i, n, kcol)),
            pl.BlockSpec((1, WINDOW, LANES), lambda i, n: (i, prev(n), vcol)),
            pl.BlockSpec((1, WINDOW, LANES), lambda i, n: (i, n, vcol)),
            pl.BlockSpec((1, 1, SWA_Q_HEADS, WINDOW, 2 * WINDOW), lambda i, n: (i, n, 0, 0, 0)),
            pl.BlockSpec(memory_space=pltpu.SMEM),
        ],
        out_specs=pl.BlockSpec((1, WINDOW, BRANCH_WIDTH), lambda i, n: (i, n, 0)),
        out_shape=jax.ShapeDtypeStruct((b, s, BRANCH_WIDTH), BF16),
        compiler_params=_params("parallel", "parallel"),
        name="swa_attn",
    )(qkv, qkv, qkv, qkv, qkv, bias, sinks)


_CONV_HALO = 32
_CONV_CHUNK = 32


def _conv_kernel(uh_ref, uc_ref, w_ref, bdw_ref, g_ref, b_ref, o_ref, ext_ref, *, ts):
    c = CONV_CHANNELS

    def glu(u):
        return u[:, :c] * jax.nn.sigmoid(u[:, c:])

    halo = glu(uh_ref[0])
    ext_ref[0:_CONV_HALO, :] = jnp.where(pl.program_id(1) > 0, halo, 0.0)
    ext_ref[_CONV_HALO:, :] = glu(uc_ref[0])
    lead = _CONV_HALO - (CONV_WIDTH - 1)
    for r0 in range(0, ts, _CONV_CHUNK):
        acc = jnp.broadcast_to(bdw_ref[...], (_CONV_CHUNK, c))
        for j in range(CONV_WIDTH):
            start = r0 + lead + j
            acc = acc + w_ref[j:j + 1, :] * ext_ref[start:start + _CONV_CHUNK, :]
        mu = jnp.mean(acc, axis=-1, keepdims=True)
        d = acc - mu
        var = jnp.mean(d * d, axis=-1, keepdims=True)
        y = d * lax.rsqrt(var + EPS) * g_ref[...] + b_ref[...]
        o_ref[0, r0:r0 + _CONV_CHUNK, :] = (y * jax.nn.sigmoid(y)).astype(o_ref.dtype)


def _conv(u, w_dw, b_dw, g_ln, b_ln, ts=256):
    b, s, _ = u.shape
    ratio = ts // _CONV_HALO
    const = lambda i, n: (0, 0)
    return pl.pallas_call(
        functools.partial(_conv_kernel, ts=ts),
        grid=(b, s // ts),
        in_specs=[
            pl.BlockSpec((1, _CONV_HALO, _U_W), lambda i, n: (i, jnp.maximum(n * ratio - 1, 0), 0)),
            pl.BlockSpec((1, ts, _U_W), lambda i, n: (i, n, 0)),
            pl.BlockSpec((CONV_WIDTH, CONV_CHANNELS), const),
            pl.BlockSpec((1, CONV_CHANNELS), const),
            pl.BlockSpec((1, CONV_CHANNELS), const),
            pl.BlockSpec((1, CONV_CHANNELS), const),
        ],
        out_specs=pl.BlockSpec((1, ts, CONV_CHANNELS), lambda i, n: (i, n, 0)),
        out_shape=jax.ShapeDtypeStruct((b, s, CONV_CHANNELS), BF16),
        scratch_shapes=[pltpu.VMEM((ts + _CONV_HALO, CONV_CHANNELS), F32)],
        compiler_params=_params("parallel", "parallel"),
        name="conformer_conv",
    )(u, u, w_dw, b_dw, g_ln, b_ln)


def _mla_prep_kernel(c_ref, pos_ref, fpat_ref, sgn_ref, gq_ref, gkv_ref,
                     wq_ref, wqr_ref, wk_ref, wv_ref, q_ref, k_ref, v_ref):
    cq = c_ref[:, 0:MLA_Q_RANK]
    ckv = c_ref[:, MLA_Q_RANK:MLA_Q_RANK + MLA_KV_RANK]
    kpe_a = c_ref[:, MLA_Q_RANK + MLA_KV_RANK:MLA_Q_RANK + MLA_KV_RANK + LANES]
    kpe_b = c_ref[:, MLA_Q_RANK + MLA_KV_RANK + LANES:]
    ang = pos_ref[...].astype(F32) * fpat_ref[...]
    sgn = sgn_ref[...]
    cos = jnp.where(sgn != 0.0, jnp.cos(ang), 1.0)
    sin = jnp.sin(ang) * sgn
    hq = _rms(cq, gq_ref[...]).astype(BF16)
    hkv = _rms(ckv, gkv_ref[...]).astype(BF16)
    k_rope = kpe_a * cos + kpe_b * sin
    q_scale = (MLA_NOPE_DIM + MLA_ROPE_DIM) ** -0.5 * math.log2(math.e)
    cos_q = cos * q_scale
    sin_q = sin * q_scale
    for h in range(MLA_HEADS):
        sl = slice(h * LANES, (h + 1) * LANES)
        qf = jnp.dot(hq, wq_ref[:, sl], preferred_element_type=F32)
        qr = jnp.dot(hq, wqr_ref[:, sl], preferred_element_type=F32)
        q_ref[:, sl] = (qf * cos_q + qr * sin_q).astype(q_ref.dtype)
        kn = jnp.dot(hkv, wk_ref[:, sl], preferred_element_type=F32)
        k_ref[:, sl] = (kn + k_rope).astype(k_ref.dtype)
    half_block = lax.broadcasted_iota(jnp.int32, (1, v_ref.shape[1]), 1) // MLA_V_DIM
    ones = ((half_block % 4 == 1) | (half_block % 4 == 2)).astype(F32)
    v_ref[...] = (jnp.dot(hkv, wv_ref[...], preferred_element_type=F32) + ones).astype(v_ref.dtype)


def _mla_prep(cmla, pos_col, fpat, sgn, gq, gkv, wq, wqr, wk, wv, ts=512):
    t = cmla.shape[0]
    row = lambda i: (i, 0)
    const = lambda i: (0, 0)
    hw = MLA_HEADS * LANES
    return pl.pallas_call(
        _mla_prep_kernel,
        grid=(t // ts,),
        in_specs=[
            pl.BlockSpec((ts, _CMLA_W), row),
            pl.BlockSpec((ts, 1), row),
            pl.BlockSpec((1, LANES), const),
            pl.BlockSpec((1, LANES), const),
            pl.BlockSpec((1, MLA_Q_RANK), const),
            pl.BlockSpec((1, MLA_KV_RANK), const),
            pl.BlockSpec((MLA_Q_RANK, hw), const),
            pl.BlockSpec((MLA_Q_RANK, hw), const),
            pl.BlockSpec((MLA_KV_RANK, hw), const),
            pl.BlockSpec((MLA_KV_RANK, hw), const),
        ],
        out_specs=[
            pl.BlockSpec((ts, hw), row),
            pl.BlockSpec((ts, hw), row),
            pl.BlockSpec((ts, hw), row),
        ],
        out_shape=[
            jax.ShapeDtypeStruct((t, hw), BF16),
            jax.ShapeDtypeStruct((t, hw), BF16),
            jax.ShapeDtypeStruct((t, hw), BF16),
        ],
        compiler_params=_params("parallel"),
        name="mla_prep",
    )(cmla, pos_col, fpat, sgn, gq, gkv, wq, wqr, wk, wv)


def _mla_attn_kernel(q_ref, k_ref, v_ref, o_ref, sa_ref, sb_ref, m_ref, acc_ref, *, t):
    i = pl.program_id(2)
    m_ref[...] = jnp.full(m_ref.shape, NEG_INF, F32)
    acc_ref[...] = jnp.zeros(acc_ref.shape, F32)

    def scores(j, s_ref):
        k0 = pl.multiple_of(j * t, t)
        for hh in range(2):
            lanes = slice(hh * LANES, (hh + 1) * LANES)
            s_ref[hh] = lax.dot_general(q_ref[0, :, lanes], k_ref[0, pl.ds(k0, t), lanes],
                                        (((1,), (1,)), ((), ())), preferred_element_type=F32)

    def consume(j, s_ref, diagonal):
        k0 = pl.multiple_of(j * t, t)
        for hh in range(2):
            lanes = slice(hh * LANES, (hh + 1) * LANES)
            s = s_ref[hh]
            if diagonal:
                row = lax.broadcasted_iota(jnp.int32, (t, t), 0)
                col = lax.broadcasted_iota(jnp.int32, (t, t), 1)
                s = jnp.where(col <= row, s, NEG_INF)
            m_old = m_ref[hh]
            m_new = jnp.maximum(m_old, jnp.max(s, axis=1, keepdims=True))
            alpha = jnp.exp2(m_old - m_new)
            p = jnp.exp2(s - jnp.concatenate([m_new] * (t // LANES), axis=1))
            m_ref[hh] = m_new
            acc_ref[hh] = alpha * acc_ref[hh] + jnp.dot(
                p.astype(BF16), v_ref[0, pl.ds(k0, t), lanes], preferred_element_type=F32)

    scores(0, sa_ref)

    def pair(jj, carry):
        j = 2 * jj
        scores(j + 1, sb_ref)
        consume(j, sa_ref, diagonal=False)
        scores(j + 2, sa_ref)
        consume(j + 1, sb_ref, diagonal=False)
        return carry

    lax.fori_loop(0, i // 2, pair, 0)

    @pl.when(i % 2 == 0)
    def _():
        consume(i, sa_ref, diagonal=True)

    @pl.when(i % 2 == 1)
    def _():
        scores(i, sb_ref)
        consume(i - 1, sa_ref, diagonal=False)
        consume(i, sb_ref, diagonal=True)

    lane = lax.broadcasted_iota(jnp.int32, (t, LANES), 1)
    acc = jnp.where(lane < MLA_V_DIM, acc_ref[0], acc_ref[1])
    den = jnp.where(lane < MLA_V_DIM, pltpu.roll(acc_ref[0], MLA_V_DIM, 1),
                    pltpu.roll(acc_ref[1], MLA_V_DIM, 1))
    o_ref[0] = (acc / den).astype(o_ref.dtype)


def _mla_attn(qf, kf, v, t=512):
    b, s, _ = qf.shape
    pairs = MLA_HEADS // 2
    return pl.pallas_call(
        functools.partial(_mla_attn_kernel, t=t),
        grid=(b, pairs, s // t),
        in_specs=[
            pl.BlockSpec((1, t, 2 * LANES), lambda i, p, n: (i, n, p)),
            pl.BlockSpec((1, s, 2 * LANES), lambda i, p, n: (i, 0, p)),
            pl.BlockSpec((1, s, 2 * LANES), lambda i, p, n: (i, 0, p)),
        ],
        out_specs=pl.BlockSpec((1, t, LANES), lambda i, p, n: (i, n, p)),
        out_shape=jax.ShapeDtypeStruct((b, s, MLA_HEADS * MLA_V_DIM), BF16),
        scratch_shapes=[pltpu.VMEM((2, t, t), F32), pltpu.VMEM((2, t, t), F32),
                        pltpu.VMEM((2, t, LANES), F32), pltpu.VMEM((2, t, LANES), F32)],
        compiler_params=_params("parallel", "parallel", "arbitrary"),
        name="mla_attn",
    )(qf, kf, v)


def _merge_kernel(ya_ref, yb_ref, yc_ref, gate_ref, x_ref, wb_ref, wo_ref, o_ref):
    merged = None
    for n, y_ref in enumerate((ya_ref, yb_ref, yc_ref)):
        br = jnp.dot(y_ref[...], wb_ref[n], preferred_element_type=F32)
        term = jax.nn.sigmoid(gate_ref[:, n * D_MODEL:(n + 1) * D_MODEL]) * br
        merged = term if merged is None else merged + term
    o_ref[...] = x_ref[...] + jnp.dot(merged.astype(BF16), wo_ref[...], preferred_element_type=F32)


def _merge(ya, yb, yc, gates, x2, wb, wo, tm=512):
    t = x2.shape[0]
    row = lambda i: (i, 0)
    return pl.pallas_call(
        _merge_kernel,
        grid=(t // tm,),
        in_specs=[
            pl.BlockSpec((tm, BRANCH_WIDTH), row),
            pl.BlockSpec((tm, BRANCH_WIDTH), row),
            pl.BlockSpec((tm, BRANCH_WIDTH), row),
            pl.BlockSpec((tm, _GATE_W), row),
            pl.BlockSpec((tm, D_MODEL), row),
            pl.BlockSpec((N_BRANCHES, BRANCH_WIDTH, D_MODEL), lambda i: (0, 0, 0)),
            pl.BlockSpec((D_MODEL, D_MODEL), lambda i: (0, 0)),
        ],
        out_specs=pl.BlockSpec((tm, D_MODEL), row),
        out_shape=jax.ShapeDtypeStruct((t, D_MODEL), F32),
        compiler_params=_params("parallel"),
        name="merge_out",
    )(ya, yb, yc, gates, x2, wb, wo)


def _ffn_kernel(x_ref, g_ref, wu_ref, wd_ref, o_ref, h_ref):
    @pl.when(pl.program_id(1) == 0)
    def _():
        xf = x_ref[...]
        h_ref[...] = _rms(xf, g_ref[...]).astype(BF16)
        o_ref[...] = xf

    up = jnp.dot(h_ref[...], wu_ref[...], preferred_element_type=F32)
    act = jnp.square(jnp.maximum(up, 0.0)).astype(BF16)
    o_ref[...] += jnp.dot(act, wd_ref[...], preferred_element_type=F32)


def _ffn(x2, g, wu, wd, tm=1024, tf=512):
    t = x2.shape[0]
    return pl.pallas_call(
        _ffn_kernel,
        grid=(t // tm, D_FF // tf),
        in_specs=[
            pl.BlockSpec((tm, D_MODEL), lambda i, f: (i, 0)),
            pl.BlockSpec((1, D_MODEL), lambda i, f: (0, 0)),
            pl.BlockSpec((D_MODEL, tf), lambda i, f: (0, f)),
            pl.BlockSpec((tf, D_MODEL), lambda i, f: (f, 0)),
        ],
        out_specs=pl.BlockSpec((tm, D_MODEL), lambda i, f: (i, 0)),
        out_shape=jax.ShapeDtypeStruct((t, D_MODEL), F32),
        scratch_shapes=[pltpu.VMEM((tm, D_MODEL), BF16)],
        compiler_params=_params("parallel", "arbitrary"),
        name="ffn",
    )(x2, g, wu, wd)


def _final_norm_kernel(x_ref, g_ref, o_ref):
    o_ref[...] = _rms(x_ref[...], g_ref[...])


def _final_norm(x2, g, tm=1024):
    t = x2.shape[0]
    return pl.pallas_call(
        _final_norm_kernel,
        grid=(t // tm,),
        in_specs=[pl.BlockSpec((tm, D_MODEL), lambda i: (i, 0)),
                  pl.BlockSpec((1, D_MODEL), lambda i: (0, 0))],
        out_specs=pl.BlockSpec((tm, D_MODEL), lambda i: (i, 0)),
        out_shape=jax.ShapeDtypeStruct((t, D_MODEL), F32),
        compiler_params=_params("parallel"),
        name="final_norm",
    )(x2, g)


def _pack_in_weights(w):
    q_end = _QKV_W
    u_end = q_end + _U_W
    cq_end = u_end + MLA_Q_RANK
    ckv_end = cq_end + MLA_KV_RANK
    kpe_end = ckv_end + MLA_ROPE_DIM
    half = MLA_ROPE_DIM // 2
    kpe = w[:, ckv_end:kpe_end]
    z_lo = jnp.zeros((w.shape[0], MLA_NOPE_DIM), w.dtype)
    z_hi = jnp.zeros((w.shape[0], LANES - MLA_NOPE_DIM - MLA_ROPE_DIM), w.dtype)
    kpe_a = jnp.concatenate([z_lo, kpe, z_hi], axis=1)
    kpe_b = jnp.concatenate([z_lo, kpe[:, half:], kpe[:, :half], z_hi], axis=1)
    return jnp.concatenate([w[:, :ckv_end], kpe_a, kpe_b, w[:, kpe_end:]], axis=1).astype(BF16)


def _pack_q_up(w):
    r = w.shape[0]
    half = MLA_ROPE_DIM // 2
    wh = w.reshape(r, MLA_HEADS, MLA_NOPE_DIM + MLA_ROPE_DIM)
    nope, pe = wh[..., :MLA_NOPE_DIM], wh[..., MLA_NOPE_DIM:]
    pad = jnp.zeros((r, MLA_HEADS, LANES - MLA_NOPE_DIM - MLA_ROPE_DIM), w.dtype)
    full = jnp.concatenate([nope, pe, pad], axis=-1)
    rot = jnp.concatenate([jnp.zeros_like(nope), pe[..., half:], pe[..., :half], pad], axis=-1)
    return (full.reshape(r, MLA_HEADS * LANES).astype(BF16),
            rot.reshape(r, MLA_HEADS * LANES).astype(BF16))


def _pack_kv_up(w):
    r = w.shape[0]
    wh = w.reshape(r, MLA_HEADS // 2, 2, MLA_NOPE_DIM + MLA_V_DIM)
    kn, v = wh[..., :MLA_NOPE_DIM], wh[..., MLA_NOPE_DIM:]
    zk = jnp.zeros_like(kn)
    k_full = jnp.concatenate([kn, zk], axis=-1)
    zv = jnp.zeros_like(v[:, :, 0])
    v_full = jnp.concatenate([v[:, :, 0], zv, zv, v[:, :, 1]], axis=-1)
    return (k_full.reshape(r, MLA_HEADS * LANES).astype(BF16),
            v_full.reshape(r, MLA_HEADS * LANES).astype(BF16))


def _rope_lane_patterns():
    half = MLA_ROPE_DIM // 2
    freqs = jnp.exp(-math.log(ROPE_THETA) * jnp.arange(half, dtype=F32) / half)
    zeros_lo = jnp.zeros((MLA_NOPE_DIM,), F32)
    zeros_hi = jnp.zeros((LANES - MLA_NOPE_DIM - MLA_ROPE_DIM,), F32)
    fpat = jnp.concatenate([zeros_lo, freqs, freqs, zeros_hi]).reshape(1, LANES)
    ones = jnp.ones((half,), F32)
    sgn = jnp.concatenate([zeros_lo, -ones, ones, zeros_hi]).reshape(1, LANES)
    return fpat, sgn


def kernel(x, positions, rel_bias, g_final, g_mix, w_in, swa_sinks, g_q_norm, w_q_up, g_kv_norm, w_kv_up, w_dw, b_dw, g_conv_ln, b_conv_ln, w_branch, w_out, g_mlp, w_up, w_down):
    b, s, d = x.shape
    t = b * s
    x2 = x.reshape(t, d)
    pos_col = positions.reshape(t, 1)
    fpat, sgn = _rope_lane_patterns()
    bias = _swa_bias(positions, rel_bias)
    for l in range(DEPTH):
        qkv, u, cmla, gates = _in_proj(x2, g_mix[l].reshape(1, d), _pack_in_weights(w_in[l]))
        y_a = _swa(qkv.reshape(b, s, _QKV_W), bias, swa_sinks[l])
        y_b = _conv(u.reshape(b, s, _U_W), w_dw[l].reshape(CONV_WIDTH, CONV_CHANNELS),
                    b_dw[l].reshape(1, -1), g_conv_ln[l].reshape(1, -1), b_conv_ln[l].reshape(1, -1))
        wq, wqr = _pack_q_up(w_q_up[l])
        wk, wv = _pack_kv_up(w_kv_up[l])
        qf, kf, v = _mla_prep(cmla, pos_col, fpat, sgn, g_q_norm[l].reshape(1, -1),
                              g_kv_norm[l].reshape(1, -1), wq, wqr, wk, wv)
        hw = MLA_HEADS * LANES
        y_c = _mla_attn(qf.reshape(b, s, hw), kf.reshape(b, s, hw),
                        v.reshape(b, s, hw))
        x2 = _merge(y_a.reshape(t, -1), y_b.reshape(t, -1), y_c.reshape(t, -1), gates, x2,
                    w_branch[l].astype(BF16), w_out[l].astype(BF16))
        x2 = _ffn(x2, g_mlp[l].reshape(1, d), w_up[l].astype(BF16), w_down[l].astype(BF16))
    return _final_norm(x2, g_final.reshape(1, d)).reshape(b, s, d)
```

```python
import functools
import math

import jax
import jax.numpy as jnp
from jax import lax
from jax.experimental import pallas as pl
from jax.experimental.pallas import tpu as pltpu

D_MODEL = 1024
DEPTH = 4
BRANCH_WIDTH = 512
N_BRANCHES = 3
SWA_Q_HEADS = 8
SWA_KV_HEADS = 2
SWA_GROUP = SWA_Q_HEADS // SWA_KV_HEADS
SWA_HEAD_DIM = 64
WINDOW = 128
N_BUCKETS = 32
MAX_DISTANCE = 128
CONV_CHANNELS = 512
CONV_WIDTH = 31
MLA_HEADS = 8
MLA_Q_RANK = 256
MLA_KV_RANK = 128
MLA_NOPE_DIM = 64
MLA_ROPE_DIM = 32
MLA_V_DIM = 64
ROPE_THETA = 10000.0
D_FF = 4 * D_MODEL
EPS = 1e-6
NEG_INF = -1e30

LANES = 128
SUBLANES = 8
VMEM_LIMIT = 56 * 1024 * 1024

F32 = jnp.float32
BF16 = jnp.bfloat16

_MAX_EXACT = N_BUCKETS // 2
_T5_THRESHOLDS = tuple(range(1, _MAX_EXACT + 1)) + tuple(
    math.ceil(_MAX_EXACT * (MAX_DISTANCE / _MAX_EXACT) ** (j / (N_BUCKETS - _MAX_EXACT)))
    for j in range(1, N_BUCKETS - _MAX_EXACT))

_QKV_W = 768
_U_W = 2 * CONV_CHANNELS
_CMLA_W = MLA_Q_RANK + MLA_KV_RANK + 2 * LANES
_GATE_W = N_BRANCHES * D_MODEL
_IN_W = _QKV_W + _U_W + _CMLA_W + _GATE_W


def _params(*semantics):
    return pltpu.CompilerParams(dimension_semantics=semantics, vmem_limit_bytes=VMEM_LIMIT)


def _rms(xf, g):
    return xf * lax.rsqrt(jnp.mean(xf * xf, axis=-1, keepdims=True) + EPS) * g


def _bias_kernel(posq_ref, pkp_ref, pkc_ref, tbl_ref, out_ref):
    blk = pl.program_id(1)
    pk_prev = jnp.where(blk > 0, pkp_ref[0], 0)
    pk = jnp.concatenate([pk_prev, pkc_ref[0]], axis=1)
    ki = lax.broadcasted_iota(jnp.int32, (SUBLANES, 2 * WINDOW), 1)
    has_prev = (blk > 0) | (ki >= WINDOW)

    dist = lax.broadcasted_iota(jnp.int32, (SUBLANES, LANES), 1)
    by_dist = [tbl_ref[pl.ds(h, 1), 0:LANES] + jnp.zeros((SUBLANES, LANES), F32) for h in range(SWA_Q_HEADS)]
    for b in range(1, N_BUCKETS):
        ge = dist >= _T5_THRESHOLDS[b - 1]
        for h in range(SWA_Q_HEADS):
            by_dist[h] = jnp.where(ge, tbl_ref[pl.ds(b * SWA_Q_HEADS + h, 1), 0:LANES], by_dist[h])

    def rows(r, carry):
        r0 = pl.multiple_of(r * SUBLANES, SUBLANES)
        pq = posq_ref[0, pl.ds(r0, SUBLANES), :]
        n = jnp.minimum(jnp.maximum(pq - pk, 0), LANES - 1)
        qi = WINDOW + r0 + lax.broadcasted_iota(jnp.int32, (SUBLANES, 2 * WINDOW), 0)
        mask = (ki <= qi) & (qi - ki < WINDOW) & has_prev
        for h in range(SWA_Q_HEADS):
            val = jnp.concatenate(
                [jnp.take_along_axis(by_dist[h], n[:, half * LANES:(half + 1) * LANES], axis=1)
                 for half in range(2 * WINDOW // LANES)], axis=1)
            out_ref[0, 0, h, pl.ds(r0, SUBLANES), :] = jnp.where(mask, val, NEG_INF)
        return carry

    lax.fori_loop(0, WINDOW // SUBLANES, rows, 0, unroll=4)


def _swa_bias(positions, rel_bias):
    b, s = positions.shape
    nb = s // WINDOW
    tbl = jnp.broadcast_to(rel_bias.astype(F32).reshape(N_BUCKETS * SWA_Q_HEADS, 1),
                           (N_BUCKETS * SWA_Q_HEADS, 2 * WINDOW))
    pos_col = positions.reshape(b, s, 1)
    pos_row = positions.reshape(b, 1, s)
    return pl.pallas_call(
        _bias_kernel,
        grid=(b, nb),
        in_specs=[
            pl.BlockSpec((1, WINDOW, 1), lambda i, n: (i, n, 0)),
            pl.BlockSpec((1, 1, WINDOW), lambda i, n: (i, 0, jnp.maximum(n - 1, 0))),
            pl.BlockSpec((1, 1, WINDOW), lambda i, n: (i, 0, n)),
            pl.BlockSpec((N_BUCKETS * SWA_Q_HEADS, 2 * WINDOW), lambda i, n: (0, 0)),
        ],
        out_specs=pl.BlockSpec((1, 1, SWA_Q_HEADS, WINDOW, 2 * WINDOW), lambda i, n: (i, n, 0, 0, 0)),
        out_shape=jax.ShapeDtypeStruct((b, nb, SWA_Q_HEADS, WINDOW, 2 * WINDOW), F32),
        compiler_params=_params("parallel", "parallel"),
        name="swa_bias",
    )(pos_col, pos_row, pos_row, tbl)


def _in_proj_kernel(x_ref, g_ref, w_ref, qkv_ref, u_ref, c_ref, gate_ref, *, chunk):
    h = _rms(x_ref[...], g_ref[...]).astype(BF16)
    col = 0
    for out_ref in (qkv_ref, u_ref, c_ref, gate_ref):
        width = out_ref.shape[1]
        for c0 in range(0, width, chunk):
            c1 = min(c0 + chunk, width)
            y = jnp.dot(h, w_ref[:, col + c0:col + c1], preferred_element_type=F32)
            if out_ref is gate_ref:
                y = jax.nn.sigmoid(y)
            out_ref[:, c0:c1] = y.astype(out_ref.dtype)
        col += width


def _in_proj(x2, g, w, tm=512):
    t = x2.shape[0]
    row = lambda i: (i, 0)
    const = lambda i: (0, 0)
    return pl.pallas_call(
        functools.partial(_in_proj_kernel, chunk=512),
        grid=(t // tm,),
        in_specs=[
            pl.BlockSpec((tm, D_MODEL), row),
            pl.BlockSpec((1, D_MODEL), const),
            pl.BlockSpec((D_MODEL, _IN_W), const),
        ],
        out_specs=[
            pl.BlockSpec((tm, _QKV_W), row),
            pl.BlockSpec((tm, _U_W), row),
            pl.BlockSpec((tm, _CMLA_W), row),
            pl.BlockSpec((tm, _GATE_W), row),
        ],
        out_shape=[
            jax.ShapeDtypeStruct((t, _QKV_W), BF16),
            jax.ShapeDtypeStruct((t, _U_W), F32),
            jax.ShapeDtypeStruct((t, _CMLA_W), F32),
            jax.ShapeDtypeStruct((t, _GATE_W), BF16),
        ],
        compiler_params=_params("parallel"),
        name="in_proj",
    )(x2, g, w)


def _swa_kernel(q_ref, kp_ref, kc_ref, vp_ref, vc_ref, bias_ref, sink_ref, o_ref):
    k = jnp.concatenate([kp_ref[0], kc_ref[0]], axis=0)
    v = jnp.concatenate([vp_ref[0], vc_ref[0]], axis=0)
    scale = SWA_HEAD_DIM ** -0.5
    outs = []
    for h in range(SWA_Q_HEADS):
        hk = h // SWA_GROUP
        qh = q_ref[0, :, h * SWA_HEAD_DIM:(h + 1) * SWA_HEAD_DIM]
        kh = k[:, hk * SWA_HEAD_DIM:(hk + 1) * SWA_HEAD_DIM]
        vh = v[:, hk * SWA_HEAD_DIM:(hk + 1) * SWA_HEAD_DIM]
        s = lax.dot_general(qh, kh, (((1,), (1,)), ((), ())), preferred_element_type=F32)
        s = s * scale + bias_ref[0, 0, h]
        sink = sink_ref[h]
        m = jnp.maximum(jnp.max(s, axis=1, keepdims=True), sink)
        p = jnp.exp(s - m)
        denom = jnp.sum(p, axis=1, keepdims=True) + jnp.exp(sink - m)
        o = jnp.dot(p.astype(BF16), vh, preferred_element_type=F32)
        outs.append(o / denom)
    o_ref[0] = jnp.concatenate(outs, axis=1).astype(o_ref.dtype)


def _swa(qkv, bias, sinks):
    b, s, _ = qkv.shape
    nb = s // WINDOW
    kcol = (SWA_Q_HEADS * SWA_HEAD_DIM) // LANES
    vcol = kcol + 1
    prev = lambda n: jnp.maximum(n - 1, 0)
    return pl.pallas_call(
        _swa_kernel,
        grid=(b, nb),
        in_specs=[
            pl.BlockSpec((1, WINDOW, SWA_Q_HEADS * SWA_HEAD_DIM), lambda i, n: (i, n, 0)),
            pl.BlockSpec((1, WINDOW, LANES), lambda i, n: (i, prev(n), kcol)),
            pl.BlockSpec((1, WINDOW, LANES), lambda i, n: (i, n, kcol)),
            pl.BlockSpec((1, WINDOW, LANES), lambda i, n: (i, prev(n), vcol)),
            pl.BlockSpec((1, WINDOW, LANES), lambda i, n: (i, n, vcol)),
            pl.BlockSpec((1, 1, SWA_Q_HEADS, WINDOW, 2 * WINDOW), lambda i, n: (i, n, 0, 0, 0)),
            pl.BlockSpec(memory_space=pltpu.SMEM),
        ],
        out_specs=pl.BlockSpec((1, WINDOW, BRANCH_WIDTH), lambda i, n: (i, n, 0)),
        out_shape=jax.ShapeDtypeStruct((b, s, BRANCH_WIDTH), BF16),
        compiler_params=_params("parallel", "parallel"),
        name="swa_attn",
    )(qkv, qkv, qkv, qkv, qkv, bias, sinks)


_CONV_HALO = 32
_CONV_CHUNK = 32


def _conv_kernel(uh_ref, uc_ref, w_ref, bdw_ref, g_ref, b_ref, o_ref, ext_ref, sh_ref, *, ts):
    c = CONV_CHANNELS

    def glu(u):
        return u[:, :c] * jax.nn.sigmoid(u[:, c:])

    halo = glu(uh_ref[0])
    ext_ref[0:_CONV_HALO, :] = jnp.where(pl.program_id(1) > 0, halo, 0.0)
    ext_ref[_CONV_HALO:, :] = glu(uc_ref[0])
    span = sh_ref.shape[1]
    for r in range(1, SUBLANES):
        sh_ref[r - 1] = ext_ref[r:r + span, :]
    lead = _CONV_HALO - (CONV_WIDTH - 1)
    for r0 in range(0, ts, _CONV_CHUNK):
        acc = jnp.broadcast_to(bdw_ref[...], (_CONV_CHUNK, c))
        for j in range(CONV_WIDTH):
            aligned, r = divmod(lead + j, SUBLANES)
            start = r0 + aligned * SUBLANES
            if r == 0:
                tap = ext_ref[start:start + _CONV_CHUNK, :]
            else:
                tap = sh_ref[r - 1, start:start + _CONV_CHUNK, :]
            acc = acc + w_ref[j:j + 1, :] * tap
        mu = jnp.mean(acc, axis=-1, keepdims=True)
        d = acc - mu
        var = jnp.mean(d * d, axis=-1, keepdims=True)
        y = d * lax.rsqrt(var + EPS) * g_ref[...] + b_ref[...]
        o_ref[0, r0:r0 + _CONV_CHUNK, :] = (y * jax.nn.sigmoid(y)).astype(o_ref.dtype)


def _conv(u, w_dw, b_dw, g_ln, b_ln, ts=256):
    b, s, _ = u.shape
    ratio = ts // _CONV_HALO
    const = lambda i, n: (0, 0)
    return pl.pallas_call(
        functools.partial(_conv_kernel, ts=ts),
        grid=(b, s // ts),
        in_specs=[
            pl.BlockSpec((1, _CONV_HALO, _U_W), lambda i, n: (i, jnp.maximum(n * ratio - 1, 0), 0)),
            pl.BlockSpec((1, ts, _U_W), lambda i, n: (i, n, 0)),
            pl.BlockSpec((CONV_WIDTH, CONV_CHANNELS), const),
            pl.BlockSpec((1, CONV_CHANNELS), const),
            pl.BlockSpec((1, CONV_CHANNELS), const),
            pl.BlockSpec((1, CONV_CHANNELS), const),
        ],
        out_specs=pl.BlockSpec((1, ts, CONV_CHANNELS), lambda i, n: (i, n, 0)),
        out_shape=jax.ShapeDtypeStruct((b, s, CONV_CHANNELS), BF16),
        scratch_shapes=[pltpu.VMEM((ts + _CONV_HALO, CONV_CHANNELS), F32),
                        pltpu.VMEM((SUBLANES - 1, ts + _CONV_HALO - SUBLANES, CONV_CHANNELS), F32)],
        compiler_params=_params("parallel", "parallel"),
        name="conformer_conv",
    )(u, u, w_dw, b_dw, g_ln, b_ln)


def _mla_prep_kernel(c_ref, pos_ref, fpat_ref, sgn_ref, gq_ref, gkv_ref,
                     wq_ref, wqr_ref, wk_ref, wv_ref, q_ref, k_ref, v_ref):
    cq = c_ref[:, 0:MLA_Q_RANK]
    ckv = c_ref[:, MLA_Q_RANK:MLA_Q_RANK + MLA_KV_RANK]
    kpe_a = c_ref[:, MLA_Q_RANK + MLA_KV_RANK:MLA_Q_RANK + MLA_KV_RANK + LANES]
    kpe_b = c_ref[:, MLA_Q_RANK + MLA_KV_RANK + LANES:]
    ang = pos_ref[...].astype(F32) * fpat_ref[...]
    sgn = sgn_ref[...]
    cos = jnp.where(sgn != 0.0, jnp.cos(ang), 1.0)
    sin = jnp.sin(ang) * sgn
    hq = _rms(cq, gq_ref[...]).astype(BF16)
    hkv = _rms(ckv, gkv_ref[...]).astype(BF16)
    k_rope = kpe_a * cos + kpe_b * sin
    q_scale = (MLA_NOPE_DIM + MLA_ROPE_DIM) ** -0.5 * math.log2(math.e)
    cos_q = cos * q_scale
    sin_q = sin * q_scale
    for h in range(MLA_HEADS):
        sl = slice(h * LANES, (h + 1) * LANES)
        qf = jnp.dot(hq, wq_ref[:, sl], preferred_element_type=F32)
        qr = jnp.dot(hq, wqr_ref[:, sl], preferred_element_type=F32)
        q_ref[:, sl] = (qf * cos_q + qr * sin_q).astype(q_ref.dtype)
        kn = jnp.dot(hkv, wk_ref[:, sl], preferred_element_type=F32)
        k_ref[:, sl] = (kn + k_rope).astype(k_ref.dtype)
    half_block = lax.broadcasted_iota(jnp.int32, (1, v_ref.shape[1]), 1) // MLA_V_DIM
    ones = ((half_block % 4 == 1) | (half_block % 4 == 2)).astype(F32)
    v_ref[...] = (jnp.dot(hkv, wv_ref[...], preferred_element_type=F32) + ones).astype(v_ref.dtype)


def _mla_prep(cmla, pos_col, fpat, sgn, gq, gkv, wq, wqr, wk, wv, ts=512):
    t = cmla.shape[0]
    row = lambda i: (i, 0)
    const = lambda i: (0, 0)
    hw = MLA_HEADS * LANES
    return pl.pallas_call(
        _mla_prep_kernel,
        grid=(t // ts,),
        in_specs=[
            pl.BlockSpec((ts, _CMLA_W), row),
            pl.BlockSpec((ts, 1), row),
            pl.BlockSpec((1, LANES), const),
            pl.BlockSpec((1, LANES), const),
            pl.BlockSpec((1, MLA_Q_RANK), const),
            pl.BlockSpec((1, MLA_KV_RANK), const),
            pl.BlockSpec((MLA_Q_RANK, hw), const),
            pl.BlockSpec((MLA_Q_RANK, hw), const),
            pl.BlockSpec((MLA_KV_RANK, hw), const),
            pl.BlockSpec((MLA_KV_RANK, hw), const),
        ],
        out_specs=[
            pl.BlockSpec((ts, hw), row),
            pl.BlockSpec((ts, hw), row),
            pl.BlockSpec((ts, hw), row),
        ],
        out_shape=[
            jax.ShapeDtypeStruct((t, hw), BF16),
            jax.ShapeDtypeStruct((t, hw), BF16),
            jax.ShapeDtypeStruct((t, hw), BF16),
        ],
        compiler_params=_params("parallel"),
        name="mla_prep",
    )(cmla, pos_col, fpat, sgn, gq, gkv, wq, wqr, wk, wv)


def _mla_attn_kernel(q_ref, k_ref, v_ref, o_ref, sa_ref, sb_ref, m_ref, acc_ref, *, t):
    i = pl.program_id(2)
    m_ref[...] = jnp.full(m_ref.shape, NEG_INF, F32)
    acc_ref[...] = jnp.zeros(acc_ref.shape, F32)

    def scores(j, s_ref):
        k0 = pl.multiple_of(j * t, t)
        for hh in range(2):
            lanes = slice(hh * LANES, (hh + 1) * LANES)
            s_ref[hh] = lax.dot_general(q_ref[0, :, lanes], k_ref[0, pl.ds(k0, t), lanes],
                                        (((1,), (1,)), ((), ())), preferred_element_type=F32)

    def consume(j, s_ref, diagonal):
        k0 = pl.multiple_of(j * t, t)
        for hh in range(2):
            lanes = slice(hh * LANES, (hh + 1) * LANES)
            s = s_ref[hh]
            if diagonal:
                row = lax.broadcasted_iota(jnp.int32, (t, t), 0)
                col = lax.broadcasted_iota(jnp.int32, (t, t), 1)
                s = jnp.where(col <= row, s, NEG_INF)
            m_old = m_ref[hh]
            m_new = jnp.maximum(m_old, jnp.max(s, axis=1, keepdims=True))
            alpha = jnp.exp2(m_old - m_new)
            p = jnp.exp2(s - jnp.concatenate([m_new] * (t // LANES), axis=1))
            m_ref[hh] = m_new
            acc_ref[hh] = alpha * acc_ref[hh] + jnp.dot(
                p.astype(BF16), v_ref[0, pl.ds(k0, t), lanes], preferred_element_type=F32)

    scores(0, sa_ref)

    def pair(jj, carry):
        j = 2 * jj
        scores(j + 1, sb_ref)
        consume(j, sa_ref, diagonal=False)
        scores(j + 2, sa_ref)
        consume(j + 1, sb_ref, diagonal=False)
        return carry

    lax.fori_loop(0, i // 2, pair, 0)

    @pl.when(i % 2 == 0)
    def _():
        consume(i, sa_ref, diagonal=True)

    @pl.when(i % 2 == 1)
    def _():
        scores(i, sb_ref)
        consume(i - 1, sa_ref, diagonal=False)
        consume(i, sb_ref, diagonal=True)

    lane = lax.broadcasted_iota(jnp.int32, (t, LANES), 1)
    acc = jnp.where(lane < MLA_V_DIM, acc_ref[0], acc_ref[1])
    den = jnp.where(lane < MLA_V_DIM, pltpu.roll(acc_ref[0], MLA_V_DIM, 1),
                    pltpu.roll(acc_ref[1], MLA_V_DIM, 1))
    o_ref[0] = (acc / den).astype(o_ref.dtype)


def _mla_attn(qf, kf, v, t=512):
    b, s, _ = qf.shape
    pairs = MLA_HEADS // 2
    return pl.pallas_call(
        functools.partial(_mla_attn_kernel, t=t),
        grid=(b, pairs, s // t),
        in_specs=[
            pl.BlockSpec((1, t, 2 * LANES), lambda i, p, n: (i, n, p)),
            pl.BlockSpec((1, s, 2 * LANES), lambda i, p, n: (i, 0, p)),
            pl.BlockSpec((1, s, 2 * LANES), lambda i, p, n: (i, 0, p)),
        ],
        out_specs=pl.BlockSpec((1, t, LANES), lambda i, p, n: (i, n, p)),
        out_shape=jax.ShapeDtypeStruct((b, s, MLA_HEADS * MLA_V_DIM), BF16),
        scratch_shapes=[pltpu.VMEM((2, t, t), F32), pltpu.VMEM((2, t, t), F32),
                        pltpu.VMEM((2, t, LANES), F32), pltpu.VMEM((2, t, LANES), F32)],
        compiler_params=_params("parallel", "parallel", "arbitrary"),
        name="mla_attn",
    )(qf, kf, v)


def _merge_kernel(ya_ref, yb_ref, yc_ref, gate_ref, x_ref, wb_ref, wo_ref, o_ref):
    merged = None
    for n, y_ref in enumerate((ya_ref, yb_ref, yc_ref)):
        br = jnp.dot(y_ref[...], wb_ref[n], preferred_element_type=F32)
        term = gate_ref[:, n * D_MODEL:(n + 1) * D_MODEL].astype(F32) * br
        merged = term if merged is None else merged + term
    o_ref[...] = x_ref[...] + jnp.dot(merged.astype(BF16), wo_ref[...], preferred_element_type=F32)


def _merge(ya, yb, yc, gates, x2, wb, wo, tm=512):
    t = x2.shape[0]
    row = lambda i: (i, 0)
    return pl.pallas_call(
        _merge_kernel,
        grid=(t // tm,),
        in_specs=[
            pl.BlockSpec((tm, BRANCH_WIDTH), row),
            pl.BlockSpec((tm, BRANCH_WIDTH), row),
            pl.BlockSpec((tm, BRANCH_WIDTH), row),
            pl.BlockSpec((tm, _GATE_W), row),
            pl.BlockSpec((tm, D_MODEL), row),
            pl.BlockSpec((N_BRANCHES, BRANCH_WIDTH, D_MODEL), lambda i: (0, 0, 0)),
            pl.BlockSpec((D_MODEL, D_MODEL), lambda i: (0, 0)),
        ],
        out_specs=pl.BlockSpec((tm, D_MODEL), row),
        out_shape=jax.ShapeDtypeStruct((t, D_MODEL), F32),
        compiler_params=_params("parallel"),
        name="merge_out",
    )(ya, yb, yc, gates, x2, wb, wo)


def _ffn_kernel(x_ref, g_ref, wu_ref, wd_ref, o_ref, h_ref):
    @pl.when(pl.program_id(1) == 0)
    def _():
        xf = x_ref[...]
        h_ref[...] = _rms(xf, g_ref[...]).astype(BF16)
        o_ref[...] = xf

    up = jnp.dot(h_ref[...], wu_ref[...], preferred_element_type=F32)
    act = jnp.square(jnp.maximum(up, 0.0)).astype(BF16)
    o_ref[...] += jnp.dot(act, wd_ref[...], preferred_element_type=F32)


def _ffn(x2, g, wu, wd, tm=1024, tf=1024):
    t = x2.shape[0]
    return pl.pallas_call(
        _ffn_kernel,
        grid=(t // tm, D_FF // tf),
        in_specs=[
            pl.BlockSpec((tm, D_MODEL), lambda i, f: (i, 0)),
            pl.BlockSpec((1, D_MODEL), lambda i, f: (0, 0)),
            pl.BlockSpec((D_MODEL, tf), lambda i, f: (0, f)),
            pl.BlockSpec((tf, D_MODEL), lambda i, f: (f, 0)),
        ],
        out_specs=pl.BlockSpec((tm, D_MODEL), lambda i, f: (i, 0)),
        out_shape=jax.ShapeDtypeStruct((t, D_MODEL), F32),
        scratch_shapes=[pltpu.VMEM((tm, D_MODEL), BF16)],
        compiler_params=_params("parallel", "arbitrary"),
        name="ffn",
    )(x2, g, wu, wd)


def _final_norm_kernel(x_ref, g_ref, o_ref):
    o_ref[...] = _rms(x_ref[...], g_ref[...])


def _final_norm(x2, g, tm=1024):
    t = x2.shape[0]
    return pl.pallas_call(
        _final_norm_kernel,
        grid=(t // tm,),
        in_specs=[pl.BlockSpec((tm, D_MODEL), lambda i: (i, 0)),
                  pl.BlockSpec((1, D_MODEL), lambda i: (0, 0))],
        out_specs=pl.BlockSpec((tm, D_MODEL), lambda i: (i, 0)),
        out_shape=jax.ShapeDtypeStruct((t, D_MODEL), F32),
        compiler_params=_params("parallel"),
        name="final_norm",
    )(x2, g)


def _pack_in_weights(w):
    q_end = _QKV_W
    u_end = q_end + _U_W
    cq_end = u_end + MLA_Q_RANK
    ckv_end = cq_end + MLA_KV_RANK
    kpe_end = ckv_end + MLA_ROPE_DIM
    half = MLA_ROPE_DIM // 2
    kpe = w[:, ckv_end:kpe_end]
    z_lo = jnp.zeros((w.shape[0], MLA_NOPE_DIM), w.dtype)
    z_hi = jnp.zeros((w.shape[0], LANES - MLA_NOPE_DIM - MLA_ROPE_DIM), w.dtype)
    kpe_a = jnp.concatenate([z_lo, kpe, z_hi], axis=1)
    kpe_b = jnp.concatenate([z_lo, kpe[:, half:], kpe[:, :half], z_hi], axis=1)
    return jnp.concatenate([w[:, :ckv_end], kpe_a, kpe_b, w[:, kpe_end:]], axis=1).astype(BF16)


def _pack_q_up(w):
    r = w.shape[0]
    half = MLA_ROPE_DIM // 2
    wh = w.reshape(r, MLA_HEADS, MLA_NOPE_DIM + MLA_ROPE_DIM)
    nope, pe = wh[..., :MLA_NOPE_DIM], wh[..., MLA_NOPE_DIM:]
    pad = jnp.zeros((r, MLA_HEADS, LANES - MLA_NOPE_DIM - MLA_ROPE_DIM), w.dtype)
    full = jnp.concatenate([nope, pe, pad], axis=-1)
    rot = jnp.concatenate([jnp.zeros_like(nope), pe[..., half:], pe[..., :half], pad], axis=-1)
    return (full.reshape(r, MLA_HEADS * LANES).astype(BF16),
            rot.reshape(r, MLA_HEADS * LANES).astype(BF16))


def _pack_kv_up(w):
    r = w.shape[0]
    wh = w.reshape(r, MLA_HEADS // 2, 2, MLA_NOPE_DIM + MLA_V_DIM)
    kn, v = wh[..., :MLA_NOPE_DIM], wh[..., MLA_NOPE_DIM:]
    zk = jnp.zeros_like(kn)
    k_full = jnp.concatenate([kn, zk], axis=-1)
    zv = jnp.zeros_like(v[:, :, 0])
    v_full = jnp.concatenate([v[:, :, 0], zv, zv, v[:, :, 1]], axis=-1)
    return (k_full.reshape(r, MLA_HEADS * LANES).astype(BF16),
            v_full.reshape(r, MLA_HEADS * LANES).astype(BF16))


def _rope_lane_patterns():
    half = MLA_ROPE_DIM // 2
    freqs = jnp.exp(-math.log(ROPE_THETA) * jnp.arange(half, dtype=F32) / half)
    zeros_lo = jnp.zeros((MLA_NOPE_DIM,), F32)
    zeros_hi = jnp.zeros((LANES - MLA_NOPE_DIM - MLA_ROPE_DIM,), F32)
    fpat = jnp.concatenate([zeros_lo, freqs, freqs, zeros_hi]).reshape(1, LANES)
    ones = jnp.ones((half,), F32)
    sgn = jnp.concatenate([zeros_lo, -ones, ones, zeros_hi]).reshape(1, LANES)
    return fpat, sgn


def kernel(x, positions, rel_bias, g_final, g_mix, w_in, swa_sinks, g_q_norm, w_q_up, g_kv_norm, w_kv_up, w_dw, b_dw, g_conv_ln, b_conv_ln, w_branch, w_out, g_mlp, w_up, w_down):
    b, s, d = x.shape
    t = b * s
    x2 = x.reshape(t, d)
    pos_col = positions.reshape(t, 1)
    fpat, sgn = _rope_lane_patterns()
    bias = _swa_bias(positions, rel_bias)
    for l in range(DEPTH):
        qkv, u, cmla, gates = _in_proj(x2, g_mix[l].reshape(1, d), _pack_in_weights(w_in[l]))
        y_a = _swa(qkv.reshape(b, s, _QKV_W), bias, swa_sinks[l])
        y_b = _conv(u.reshape(b, s, _U_W), w_dw[l].reshape(CONV_WIDTH, CONV_CHANNELS),
                    b_dw[l].reshape(1, -1), g_conv_ln[l].reshape(1, -1), b_conv_ln[l].reshape(1, -1))
        wq, wqr = _pack_q_up(w_q_up[l])
        wk, wv = _pack_kv_up(w_kv_up[l])
        qf, kf, v = _mla_prep(cmla, pos_col, fpat, sgn, g_q_norm[l].reshape(1, -1),
                              g_kv_norm[l].reshape(1, -1), wq, wqr, wk, wv)
        hw = MLA_HEADS * LANES
        y_c = _mla_attn(qf.reshape(b, s, hw), kf.reshape(b, s, hw),
                        v.reshape(b, s, hw))
        x2 = _merge(y_a.reshape(t, -1), y_b.reshape(t, -1), y_c.reshape(t, -1), gates, x2,
                    w_branch[l].astype(BF16), w_out[l].astype(BF16))
        x2 = _ffn(x2, g_mlp[l].reshape(1, d), w_up[l].astype(BF16), w_down[l].astype(BF16))
    return _final_norm(x2, g_final.reshape(1, d)).reshape(b, s, d)
```

```python
import functools
import math

import jax
import jax.numpy as jnp
from jax import lax
from jax.experimental import pallas as pl
from jax.experimental.pallas import tpu as pltpu

D_MODEL = 1024
DEPTH = 4
BRANCH_WIDTH = 512
N_BRANCHES = 3
SWA_Q_HEADS = 8
SWA_KV_HEADS = 2
SWA_GROUP = SWA_Q_HEADS // SWA_KV_HEADS
SWA_HEAD_DIM = 64
WINDOW = 128
N_BUCKETS = 32
MAX_DISTANCE = 128
CONV_CHANNELS = 512
CONV_WIDTH = 31
MLA_HEADS = 8
MLA_Q_RANK = 256
MLA_KV_RANK = 128
MLA_NOPE_DIM = 64
MLA_ROPE_DIM = 32
MLA_V_DIM = 64
ROPE_THETA = 10000.0
D_FF = 4 * D_MODEL
EPS = 1e-6
NEG_INF = -1e30

LANES = 128
SUBLANES = 8
VMEM_LIMIT = 56 * 1024 * 1024

F32 = jnp.float32
BF16 = jnp.bfloat16

_MAX_EXACT = N_BUCKETS // 2
_T5_THRESHOLDS = tuple(range(1, _MAX_EXACT + 1)) + tuple(
    math.ceil(_MAX_EXACT * (MAX_DISTANCE / _MAX_EXACT) ** (j / (N_BUCKETS - _MAX_EXACT)))
    for j in range(1, N_BUCKETS - _MAX_EXACT))

_QKV_W = 768
_U_W = 2 * CONV_CHANNELS
_CMLA_W = MLA_Q_RANK + MLA_KV_RANK + 2 * LANES
_GATE_W = N_BRANCHES * D_MODEL
_IN_W = _QKV_W + _U_W + _CMLA_W + _GATE_W


def _params(*semantics):
    return pltpu.CompilerParams(dimension_semantics=semantics, vmem_limit_bytes=VMEM_LIMIT)


def _rms(xf, g):
    return xf * lax.rsqrt(jnp.mean(xf * xf, axis=-1, keepdims=True) + EPS) * g


def _bias_kernel(posq_ref, pkp_ref, pkc_ref, tbl_ref, out_ref):
    blk = pl.program_id(1)
    pk_prev = jnp.where(blk > 0, pkp_ref[0], 0)
    pk = jnp.concatenate([pk_prev, pkc_ref[0]], axis=1)
    ki = lax.broadcasted_iota(jnp.int32, (SUBLANES, 2 * WINDOW), 1)
    has_prev = (blk > 0) | (ki >= WINDOW)

    dist = lax.broadcasted_iota(jnp.int32, (SUBLANES, LANES), 1)
    by_dist = [tbl_ref[pl.ds(h, 1), 0:LANES] + jnp.zeros((SUBLANES, LANES), F32) for h in range(SWA_Q_HEADS)]
    for b in range(1, N_BUCKETS):
        ge = dist >= _T5_THRESHOLDS[b - 1]
        for h in range(SWA_Q_HEADS):
            by_dist[h] = jnp.where(ge, tbl_ref[pl.ds(b * SWA_Q_HEADS + h, 1), 0:LANES], by_dist[h])

    def rows(r, carry):
        r0 = pl.multiple_of(r * SUBLANES, SUBLANES)
        pq = posq_ref[0, pl.ds(r0, SUBLANES), :]
        n = jnp.minimum(jnp.maximum(pq - pk, 0), LANES - 1)
        qi = WINDOW + r0 + lax.broadcasted_iota(jnp.int32, (SUBLANES, 2 * WINDOW), 0)
        mask = (ki <= qi) & (qi - ki < WINDOW) & has_prev
        for h in range(SWA_Q_HEADS):
            val = jnp.concatenate(
                [jnp.take_along_axis(by_dist[h], n[:, half * LANES:(half + 1) * LANES], axis=1)
                 for half in range(2 * WINDOW // LANES)], axis=1)
            out_ref[0, 0, h, pl.ds(r0, SUBLANES), :] = jnp.where(mask, val, NEG_INF)
        return carry

    lax.fori_loop(0, WINDOW // SUBLANES, rows, 0, unroll=4)


def _swa_bias(positions, rel_bias):
    b, s = positions.shape
    nb = s // WINDOW
    tbl = jnp.broadcast_to(rel_bias.astype(F32).reshape(N_BUCKETS * SWA_Q_HEADS, 1),
                           (N_BUCKETS * SWA_Q_HEADS, 2 * WINDOW))
    pos_col = positions.reshape(b, s, 1)
    pos_row = positions.reshape(b, 1, s)
    return pl.pallas_call(
        _bias_kernel,
        grid=(b, nb),
        in_specs=[
            pl.BlockSpec((1, WINDOW, 1), lambda i, n: (i, n, 0)),
            pl.BlockSpec((1, 1, WINDOW), lambda i, n: (i, 0, jnp.maximum(n - 1, 0))),
            pl.BlockSpec((1, 1, WINDOW), lambda i, n: (i, 0, n)),
            pl.BlockSpec((N_BUCKETS * SWA_Q_HEADS, 2 * WINDOW), lambda i, n: (0, 0)),
        ],
        out_specs=pl.BlockSpec((1, 1, SWA_Q_HEADS, WINDOW, 2 * WINDOW), lambda i, n: (i, n, 0, 0, 0)),
        out_shape=jax.ShapeDtypeStruct((b, nb, SWA_Q_HEADS, WINDOW, 2 * WINDOW), F32),
        compiler_params=_params("parallel", "parallel"),
        name="swa_bias",
    )(pos_col, pos_row, pos_row, tbl)


def _in_proj_kernel(x_ref, g_ref, w_ref, pos_ref, fpat_ref, sgn_ref, gq_ref, gkv_ref,
                    wq_ref, wqr_ref, wk_ref, wv_ref,
                    qkv_ref, u_ref, gate_ref, q_ref, k_ref, v_ref, *, chunk):
    h = _rms(x_ref[...], g_ref[...]).astype(BF16)

    def project(col, width, out_ref):
        for c0 in range(0, width, chunk):
            c1 = min(c0 + chunk, width)
            out_ref[:, c0:c1] = jnp.dot(h, w_ref[:, col + c0:col + c1],
                                        preferred_element_type=F32).astype(out_ref.dtype)

    project(0, _QKV_W, qkv_ref)
    project(_QKV_W, _U_W, u_ref)
    c_col = _QKV_W + _U_W
    project(c_col + _CMLA_W, _GATE_W, gate_ref)
    c = jnp.dot(h, w_ref[:, c_col:c_col + _CMLA_W], preferred_element_type=F32)
    _mla_prep(c, pos_ref, fpat_ref, sgn_ref, gq_ref, gkv_ref, wq_ref, wqr_ref, wk_ref, wv_ref,
              q_ref, k_ref, v_ref)


def _in_proj(x2, g, w, pos_col, fpat, sgn, gq, gkv, wq, wqr, wk, wv, tm=512):
    t = x2.shape[0]
    row = lambda i: (i, 0)
    const = lambda i: (0, 0)
    hw = MLA_HEADS * LANES
    return pl.pallas_call(
        functools.partial(_in_proj_kernel, chunk=512),
        grid=(t // tm,),
        in_specs=[
            pl.BlockSpec((tm, D_MODEL), row),
            pl.BlockSpec((1, D_MODEL), const),
            pl.BlockSpec((D_MODEL, _IN_W), const),
            pl.BlockSpec((tm, 1), row),
            pl.BlockSpec((1, LANES), const),
            pl.BlockSpec((1, LANES), const),
            pl.BlockSpec((1, MLA_Q_RANK), const),
            pl.BlockSpec((1, MLA_KV_RANK), const),
            pl.BlockSpec((MLA_Q_RANK, hw), const),
            pl.BlockSpec((MLA_Q_RANK, hw), const),
            pl.BlockSpec((MLA_KV_RANK, hw), const),
            pl.BlockSpec((MLA_KV_RANK, hw), const),
        ],
        out_specs=[
            pl.BlockSpec((tm, _QKV_W), row),
            pl.BlockSpec((tm, _U_W), row),
            pl.BlockSpec((tm, _GATE_W), row),
            pl.BlockSpec((tm, hw), row),
            pl.BlockSpec((tm, hw), row),
            pl.BlockSpec((tm, hw), row),
        ],
        out_shape=[
            jax.ShapeDtypeStruct((t, _QKV_W), BF16),
            jax.ShapeDtypeStruct((t, _U_W), F32),
            jax.ShapeDtypeStruct((t, _GATE_W), BF16),
            jax.ShapeDtypeStruct((t, hw), BF16),
            jax.ShapeDtypeStruct((t, hw), BF16),
            jax.ShapeDtypeStruct((t, hw), BF16),
        ],
        compiler_params=_params("parallel"),
        name="in_proj",
    )(x2, g, w, pos_col, fpat, sgn, gq, gkv, wq, wqr, wk, wv)


def _swa_kernel(q_ref, kp_ref, kc_ref, vp_ref, vc_ref, bias_ref, sink_ref, o_ref):
    k = jnp.concatenate([kp_ref[0], kc_ref[0]], axis=0)
    v = jnp.concatenate([vp_ref[0], vc_ref[0]], axis=0)
    scale = SWA_HEAD_DIM ** -0.5
    outs = []
    for h in range(SWA_Q_HEADS):
        hk = h // SWA_GROUP
        qh = q_ref[0, :, h * SWA_HEAD_DIM:(h + 1) * SWA_HEAD_DIM]
        kh = k[:, hk * SWA_HEAD_DIM:(hk + 1) * SWA_HEAD_DIM]
        vh = v[:, hk * SWA_HEAD_DIM:(hk + 1) * SWA_HEAD_DIM]
        s = lax.dot_general(qh, kh, (((1,), (1,)), ((), ())), preferred_element_type=F32)
        s = s * scale + bias_ref[0, 0, h]
        sink = sink_ref[h]
        m = jnp.maximum(jnp.max(s, axis=1, keepdims=True), sink)
        p = jnp.exp(s - m)
        denom = jnp.sum(p, axis=1, keepdims=True) + jnp.exp(sink - m)
        o = jnp.dot(p.astype(BF16), vh, preferred_element_type=F32)
        outs.append(o / denom)
    o_ref[0] = jnp.concatenate(outs, axis=1).astype(o_ref.dtype)


def _swa(qkv, bias, sinks):
    b, s, _ = qkv.shape
    nb = s // WINDOW
    kcol = (SWA_Q_HEADS * SWA_HEAD_DIM) // LANES
    vcol = kcol + 1
    prev = lambda n: jnp.maximum(n - 1, 0)
    return pl.pallas_call(
        _swa_kernel,
        grid=(b, nb),
        in_specs=[
            pl.BlockSpec((1, WINDOW, SWA_Q_HEADS * SWA_HEAD_DIM), lambda i, n: (i, n, 0)),
            pl.BlockSpec((1, WINDOW, LANES), lambda i, n: (i, prev(n), kcol)),
            pl.BlockSpec((1, WINDOW, LANES), lambda i, n: (i, n, kcol)),
            pl.BlockSpec((1, WINDOW, LANES), lambda i, n: (i, prev(n), vcol)),
            pl.BlockSpec((1, WINDOW, LANES), lambda i, n: (i, n, vcol)),
            pl.BlockSpec((1, 1, SWA_Q_HEADS, WINDOW, 2 * WINDOW), lambda i, n: (i, n, 0, 0, 0)),
            pl.BlockSpec(memory_space=pltpu.SMEM),
        ],
        out_specs=pl.BlockSpec((1, WINDOW, BRANCH_WIDTH), lambda i, n: (i, n, 0)),
        out_shape=jax.ShapeDtypeStruct((b, s, BRANCH_WIDTH), BF16),
        compiler_params=_params("parallel", "parallel"),
        name="swa_attn",
    )(qkv, qkv, qkv, qkv, qkv, bias, sinks)


_CONV_HALO = 32
_CONV_CHUNK = 32


def _conv_kernel(uh_ref, uc_ref, w_ref, bdw_ref, g_ref, b_ref, o_ref, ext_ref, sh_ref, *, ts):
    c = CONV_CHANNELS

    def glu(u):
        return u[:, :c] * jax.nn.sigmoid(u[:, c:])

    halo = glu(uh_ref[0])
    ext_ref[0:_CONV_HALO, :] = jnp.where(pl.program_id(1) > 0, halo, 0.0)
    ext_ref[_CONV_HALO:, :] = glu(uc_ref[0])
    span = sh_ref.shape[1]
    for r in range(1, SUBLANES):
        sh_ref[r - 1] = ext_ref[r:r + span, :]
    lead = _CONV_HALO - (CONV_WIDTH - 1)
    for r0 in range(0, ts, _CONV_CHUNK):
        acc = jnp.broadcast_to(bdw_ref[...], (_CONV_CHUNK, c))
        for j in range(CONV_WIDTH):
            aligned, r = divmod(lead + j, SUBLANES)
            start = r0 + aligned * SUBLANES
            if r == 0:
                tap = ext_ref[start:start + _CONV_CHUNK, :]
            else:
                tap = sh_ref[r - 1, start:start + _CONV_CHUNK, :]
            acc = acc + w_ref[j:j + 1, :] * tap
        mu = jnp.mean(acc, axis=-1, keepdims=True)
        d = acc - mu
        var = jnp.mean(d * d, axis=-1, keepdims=True)
        y = d * lax.rsqrt(var + EPS) * g_ref[...] + b_ref[...]
        o_ref[0, r0:r0 + _CONV_CHUNK, :] = (y * jax.nn.sigmoid(y)).astype(o_ref.dtype)


def _conv(u, w_dw, b_dw, g_ln, b_ln, ts=256):
    b, s, _ = u.shape
    ratio = ts // _CONV_HALO
    const = lambda i, n: (0, 0)
    return pl.pallas_call(
        functools.partial(_conv_kernel, ts=ts),
        grid=(b, s // ts),
        in_specs=[
            pl.BlockSpec((1, _CONV_HALO, _U_W), lambda i, n: (i, jnp.maximum(n * ratio - 1, 0), 0)),
            pl.BlockSpec((1, ts, _U_W), lambda i, n: (i, n, 0)),
            pl.BlockSpec((CONV_WIDTH, CONV_CHANNELS), const),
            pl.BlockSpec((1, CONV_CHANNELS), const),
            pl.BlockSpec((1, CONV_CHANNELS), const),
            pl.BlockSpec((1, CONV_CHANNELS), const),
        ],
        out_specs=pl.BlockSpec((1, ts, CONV_CHANNELS), lambda i, n: (i, n, 0)),
        out_shape=jax.ShapeDtypeStruct((b, s, CONV_CHANNELS), BF16),
        scratch_shapes=[pltpu.VMEM((ts + _CONV_HALO, CONV_CHANNELS), F32),
                        pltpu.VMEM((SUBLANES - 1, ts + _CONV_HALO - SUBLANES, CONV_CHANNELS), F32)],
        compiler_params=_params("parallel", "parallel"),
        name="conformer_conv",
    )(u, u, w_dw, b_dw, g_ln, b_ln)


def _mla_prep(c, pos_ref, fpat_ref, sgn_ref, gq_ref, gkv_ref,
              wq_ref, wqr_ref, wk_ref, wv_ref, q_ref, k_ref, v_ref):
    cq = c[:, 0:MLA_Q_RANK]
    ckv = c[:, MLA_Q_RANK:MLA_Q_RANK + MLA_KV_RANK]
    kpe_a = c[:, MLA_Q_RANK + MLA_KV_RANK:MLA_Q_RANK + MLA_KV_RANK + LANES]
    kpe_b = c[:, MLA_Q_RANK + MLA_KV_RANK + LANES:]
    ang = pos_ref[...].astype(F32) * fpat_ref[...]
    sgn = sgn_ref[...]
    cos = jnp.where(sgn != 0.0, jnp.cos(ang), 1.0)
    sin = jnp.sin(ang) * sgn
    hq = _rms(cq, gq_ref[...]).astype(BF16)
    hkv = _rms(ckv, gkv_ref[...]).astype(BF16)
    k_rope = kpe_a * cos + kpe_b * sin
    q_scale = (MLA_NOPE_DIM + MLA_ROPE_DIM) ** -0.5 * math.log2(math.e)
    cos_q = cos * q_scale
    sin_q = sin * q_scale
    for h in range(MLA_HEADS):
        sl = slice(h * LANES, (h + 1) * LANES)
        qf = jnp.dot(hq, wq_ref[:, sl], preferred_element_type=F32)
        qr = jnp.dot(hq, wqr_ref[:, sl], preferred_element_type=F32)
        q_ref[:, sl] = (qf * cos_q + qr * sin_q).astype(q_ref.dtype)
        kn = jnp.dot(hkv, wk_ref[:, sl], preferred_element_type=F32)
        k_ref[:, sl] = (kn + k_rope).astype(k_ref.dtype)
    half_block = lax.broadcasted_iota(jnp.int32, (1, v_ref.shape[1]), 1) // MLA_V_DIM
    ones = ((half_block % 4 == 1) | (half_block % 4 == 2)).astype(F32)
    v_ref[...] = (jnp.dot(hkv, wv_ref[...], preferred_element_type=F32) + ones).astype(v_ref.dtype)


def _mla_attn_kernel(q_ref, k_ref, v_ref, o_ref, sa_ref, sb_ref, m_ref, acc_ref, *, t):
    i = pl.program_id(2)
    m_ref[...] = jnp.full(m_ref.shape, NEG_INF, F32)
    acc_ref[...] = jnp.zeros(acc_ref.shape, F32)

    def scores(j, s_ref):
        k0 = pl.multiple_of(j * t, t)
        for hh in range(2):
            lanes = slice(hh * LANES, (hh + 1) * LANES)
            s_ref[hh] = lax.dot_general(q_ref[0, :, lanes], k_ref[0, pl.ds(k0, t), lanes],
                                        (((1,), (1,)), ((), ())), preferred_element_type=F32)

    def consume(j, s_ref, diagonal):
        k0 = pl.multiple_of(j * t, t)
        for hh in range(2):
            lanes = slice(hh * LANES, (hh + 1) * LANES)
            s = s_ref[hh]
            if diagonal:
                row = lax.broadcasted_iota(jnp.int32, (t, t), 0)
                col = lax.broadcasted_iota(jnp.int32, (t, t), 1)
                s = jnp.where(col <= row, s, NEG_INF)
            m_old = m_ref[hh]
            m_new = jnp.maximum(m_old, jnp.max(s, axis=1, keepdims=True))
            alpha = jnp.exp2(m_old - m_new)
            p = jnp.exp2(s - jnp.concatenate([m_new] * (t // LANES), axis=1))
            m_ref[hh] = m_new
            acc_ref[hh] = alpha * acc_ref[hh] + jnp.dot(
                p.astype(BF16), v_ref[0, pl.ds(k0, t), lanes], preferred_element_type=F32)

    scores(0, sa_ref)

    def pair(jj, carry):
        j = 2 * jj
        scores(j + 1, sb_ref)
        consume(j, sa_ref, diagonal=False)
        scores(j + 2, sa_ref)
        consume(j + 1, sb_ref, diagonal=False)
        return carry

    lax.fori_loop(0, i // 2, pair, 0)

    @pl.when(i % 2 == 0)
    def _():
        consume(i, sa_ref, diagonal=True)

    @pl.when(i % 2 == 1)
    def _():
        scores(i, sb_ref)
        consume(i - 1, sa_ref, diagonal=False)
        consume(i, sb_ref, diagonal=True)

    lane = lax.broadcasted_iota(jnp.int32, (t, LANES), 1)
    acc = jnp.where(lane < MLA_V_DIM, acc_ref[0], acc_ref[1])
    den = jnp.where(lane < MLA_V_DIM, pltpu.roll(acc_ref[0], MLA_V_DIM, 1),
                    pltpu.roll(acc_ref[1], MLA_V_DIM, 1))
    o_ref[0] = (acc / den).astype(o_ref.dtype)


def _mla_attn(qf, kf, v, t=512):
    b, s, _ = qf.shape
    pairs = MLA_HEADS // 2
    return pl.pallas_call(
        functools.partial(_mla_attn_kernel, t=t),
        grid=(b, pairs, s // t),
        in_specs=[
            pl.BlockSpec((1, t, 2 * LANES), lambda i, p, n: (i, n, p)),
            pl.BlockSpec((1, s, 2 * LANES), lambda i, p, n: (i, 0, p)),
            pl.BlockSpec((1, s, 2 * LANES), lambda i, p, n: (i, 0, p)),
        ],
        out_specs=pl.BlockSpec((1, t, LANES), lambda i, p, n: (i, n, p)),
        out_shape=jax.ShapeDtypeStruct((b, s, MLA_HEADS * MLA_V_DIM), BF16),
        scratch_shapes=[pltpu.VMEM((2, t, t), F32), pltpu.VMEM((2, t, t), F32),
                        pltpu.VMEM((2, t, LANES), F32), pltpu.VMEM((2, t, LANES), F32)],
        compiler_params=_params("parallel", "parallel", "arbitrary"),
        name="mla_attn",
    )(qf, kf, v)


def _merge_kernel(ya_ref, yb_ref, yc_ref, gate_ref, x_ref, wb_ref, wo_ref, o_ref):
    merged = None
    for n, y_ref in enumerate((ya_ref, yb_ref, yc_ref)):
        br = jnp.dot(y_ref[...], wb_ref[n], preferred_element_type=F32)
        term = jax.nn.sigmoid(gate_ref[:, n * D_MODEL:(n + 1) * D_MODEL].astype(F32)) * br
        merged = term if merged is None else merged + term
    o_ref[...] = x_ref[...] + jnp.dot(merged.astype(BF16), wo_ref[...], preferred_element_type=F32)


def _merge(ya, yb, yc, gates, x2, wb, wo, tm=512):
    t = x2.shape[0]
    row = lambda i: (i, 0)
    return pl.pallas_call(
        _merge_kernel,
        grid=(t // tm,),
        in_specs=[
            pl.BlockSpec((tm, BRANCH_WIDTH), row),
            pl.BlockSpec((tm, BRANCH_WIDTH), row),
            pl.BlockSpec((tm, BRANCH_WIDTH), row),
            pl.BlockSpec((tm, _GATE_W), row),
            pl.BlockSpec((tm, D_MODEL), row),
            pl.BlockSpec((N_BRANCHES, BRANCH_WIDTH, D_MODEL), lambda i: (0, 0, 0)),
            pl.BlockSpec((D_MODEL, D_MODEL), lambda i: (0, 0)),
        ],
        out_specs=pl.BlockSpec((tm, D_MODEL), row),
        out_shape=jax.ShapeDtypeStruct((t, D_MODEL), F32),
        compiler_params=_params("parallel"),
        name="merge_out",
    )(ya, yb, yc, gates, x2, wb, wo)


def _ffn_kernel(x_ref, g_ref, wu_ref, wd_ref, *rest, out_norm):
    if out_norm:
        gout_ref, o_ref, h_ref = rest
    else:
        o_ref, h_ref = rest
    f = pl.program_id(1)

    @pl.when(f == 0)
    def _():
        xf = x_ref[...]
        h_ref[...] = _rms(xf, g_ref[...]).astype(BF16)
        o_ref[...] = xf

    up = jnp.dot(h_ref[...], wu_ref[...], preferred_element_type=F32)
    act = jnp.square(jnp.maximum(up, 0.0)).astype(BF16)
    o_ref[...] += jnp.dot(act, wd_ref[...], preferred_element_type=F32)

    if out_norm:
        @pl.when(f == pl.num_programs(1) - 1)
        def _():
            o_ref[...] = _rms(o_ref[...], gout_ref[...])


def _ffn(x2, g, wu, wd, g_out=None, tm=1024, tf=1024):
    t = x2.shape[0]
    vec = pl.BlockSpec((1, D_MODEL), lambda i, f: (0, 0))
    in_specs = [
        pl.BlockSpec((tm, D_MODEL), lambda i, f: (i, 0)),
        vec,
        pl.BlockSpec((D_MODEL, tf), lambda i, f: (0, f)),
        pl.BlockSpec((tf, D_MODEL), lambda i, f: (f, 0)),
    ]
    args = [x2, g, wu, wd]
    if g_out is not None:
        in_specs.append(vec)
        args.append(g_out)
    return pl.pallas_call(
        functools.partial(_ffn_kernel, out_norm=g_out is not None),
        grid=(t // tm, D_FF // tf),
        in_specs=in_specs,
        out_specs=pl.BlockSpec((tm, D_MODEL), lambda i, f: (i, 0)),
        out_shape=jax.ShapeDtypeStruct((t, D_MODEL), F32),
        scratch_shapes=[pltpu.VMEM((tm, D_MODEL), BF16)],
        compiler_params=_params("parallel", "arbitrary"),
        name="ffn",
    )(*args)


def _pack_in_weights(w):
    q_end = _QKV_W
    u_end = q_end + _U_W
    cq_end = u_end + MLA_Q_RANK
    ckv_end = cq_end + MLA_KV_RANK
    kpe_end = ckv_end + MLA_ROPE_DIM
    half = MLA_ROPE_DIM // 2
    kpe = w[:, ckv_end:kpe_end]
    z_lo = jnp.zeros((w.shape[0], MLA_NOPE_DIM), w.dtype)
    z_hi = jnp.zeros((w.shape[0], LANES - MLA_NOPE_DIM - MLA_ROPE_DIM), w.dtype)
    kpe_a = jnp.concatenate([z_lo, kpe, z_hi], axis=1)
    kpe_b = jnp.concatenate([z_lo, kpe[:, half:], kpe[:, :half], z_hi], axis=1)
    return jnp.concatenate([w[:, :ckv_end], kpe_a, kpe_b, w[:, kpe_end:]], axis=1).astype(BF16)


def _pack_q_up(w):
    r = w.shape[0]
    half = MLA_ROPE_DIM // 2
    wh = w.reshape(r, MLA_HEADS, MLA_NOPE_DIM + MLA_ROPE_DIM)
    nope, pe = wh[..., :MLA_NOPE_DIM], wh[..., MLA_NOPE_DIM:]
    pad = jnp.zeros((r, MLA_HEADS, LANES - MLA_NOPE_DIM - MLA_ROPE_DIM), w.dtype)
    full = jnp.concatenate([nope, pe, pad], axis=-1)
    rot = jnp.concatenate([jnp.zeros_like(nope), pe[..., half:], pe[..., :half], pad], axis=-1)
    return (full.reshape(r, MLA_HEADS * LANES).astype(BF16),
            rot.reshape(r, MLA_HEADS * LANES).astype(BF16))


def _pack_kv_up(w):
    r = w.shape[0]
    wh = w.reshape(r, MLA_HEADS // 2, 2, MLA_NOPE_DIM + MLA_V_DIM)
    kn, v = wh[..., :MLA_NOPE_DIM], wh[..., MLA_NOPE_DIM:]
    zk = jnp.zeros_like(kn)
    k_full = jnp.concatenate([kn, zk], axis=-1)
    zv = jnp.zeros_like(v[:, :, 0])
    v_full = jnp.concatenate([v[:, :, 0], zv, zv, v[:, :, 1]], axis=-1)
    return (k_full.reshape(r, MLA_HEADS * LANES).astype(BF16),
            v_full.reshape(r, MLA_HEADS * LANES).astype(BF16))


def _rope_lane_patterns():
    half = MLA_ROPE_DIM // 2
    freqs = jnp.exp(-math.log(ROPE_THETA) * jnp.arange(half, dtype=F32) / half)
    zeros_lo = jnp.zeros((MLA_NOPE_DIM,), F32)
    zeros_hi = jnp.zeros((LANES - MLA_NOPE_DIM - MLA_ROPE_DIM,), F32)
    fpat = jnp.concatenate([zeros_lo, freqs, freqs, zeros_hi]).reshape(1, LANES)
    ones = jnp.ones((half,), F32)
    sgn = jnp.concatenate([zeros_lo, -ones, ones, zeros_hi]).reshape(1, LANES)
    return fpat, sgn


def kernel(x, positions, rel_bias, g_final, g_mix, w_in, swa_sinks, g_q_norm, w_q_up, g_kv_norm, w_kv_up, w_dw, b_dw, g_conv_ln, b_conv_ln, w_branch, w_out, g_mlp, w_up, w_down):
    b, s, d = x.shape
    t = b * s
    x2 = x.reshape(t, d)
    pos_col = positions.reshape(t, 1)
    fpat, sgn = _rope_lane_patterns()
    bias = _swa_bias(positions, rel_bias)
    for l in range(DEPTH):
        wq, wqr = _pack_q_up(w_q_up[l])
        wk, wv = _pack_kv_up(w_kv_up[l])
        qkv, u, gates, qf, kf, v = _in_proj(
            x2, g_mix[l].reshape(1, d), _pack_in_weights(w_in[l]), pos_col, fpat, sgn,
            g_q_norm[l].reshape(1, -1), g_kv_norm[l].reshape(1, -1), wq, wqr, wk, wv)
        y_a = _swa(qkv.reshape(b, s, _QKV_W), bias, swa_sinks[l])
        y_b = _conv(u.reshape(b, s, _U_W), w_dw[l].reshape(CONV_WIDTH, CONV_CHANNELS),
                    b_dw[l].reshape(1, -1), g_conv_ln[l].reshape(1, -1), b_conv_ln[l].reshape(1, -1))
        hw = MLA_HEADS * LANES
        y_c = _mla_attn(qf.reshape(b, s, hw), kf.reshape(b, s, hw),
                        v.reshape(b, s, hw))
        x2 = _merge(y_a.reshape(t, -1), y_b.reshape(t, -1), y_c.reshape(t, -1), gates, x2,
                    w_branch[l].astype(BF16), w_out[l].astype(BF16))
        x2 = _ffn(x2, g_mlp[l].reshape(1, d), w_up[l].astype(BF16), w_down[l].astype(BF16),
                  g_out=g_final.reshape(1, d) if l == DEPTH - 1 else None)
    return x2.reshape(b, s, d)
```

```python
import functools
import math

import jax
import jax.numpy as jnp
from jax import lax
from jax.experimental import pallas as pl
from jax.experimental.pallas import tpu as pltpu

D_MODEL = 1024
DEPTH = 4
BRANCH_WIDTH = 512
N_BRANCHES = 3
SWA_Q_HEADS = 8
SWA_KV_HEADS = 2
SWA_GROUP = SWA_Q_HEADS // SWA_KV_HEADS
SWA_HEAD_DIM = 64
WINDOW = 128
N_BUCKETS = 32
MAX_DISTANCE = 128
CONV_CHANNELS = 512
CONV_WIDTH = 31
MLA_HEADS = 8
MLA_Q_RANK = 256
MLA_KV_RANK = 128
MLA_NOPE_DIM = 64
MLA_ROPE_DIM = 32
MLA_V_DIM = 64
ROPE_THETA = 10000.0
D_FF = 4 * D_MODEL
EPS = 1e-6
NEG_INF = -1e30

LANES = 128
SUBLANES = 8
VMEM_LIMIT = 56 * 1024 * 1024

F32 = jnp.float32
BF16 = jnp.bfloat16

_MAX_EXACT = N_BUCKETS // 2
_T5_THRESHOLDS = tuple(range(1, _MAX_EXACT + 1)) + tuple(
    math.ceil(_MAX_EXACT * (MAX_DISTANCE / _MAX_EXACT) ** (j / (N_BUCKETS - _MAX_EXACT)))
    for j in range(1, N_BUCKETS - _MAX_EXACT))

_QKV_W = 768
_U_W = 2 * CONV_CHANNELS
_CMLA_W = MLA_Q_RANK + MLA_KV_RANK + 2 * LANES
_GATE_W = N_BRANCHES * D_MODEL
_IN_W = _QKV_W + _U_W + _CMLA_W + _GATE_W


def _params(*semantics):
    return pltpu.CompilerParams(dimension_semantics=semantics, vmem_limit_bytes=VMEM_LIMIT)


def _rms(xf, g):
    return xf * lax.rsqrt(jnp.mean(xf * xf, axis=-1, keepdims=True) + EPS) * g


def _bias_kernel(posq_ref, pkp_ref, pkc_ref, tbl_ref, out_ref):
    blk = pl.program_id(1)
    pk_prev = jnp.where(blk > 0, pkp_ref[0], 0)
    pk = jnp.concatenate([pk_prev, pkc_ref[0]], axis=1)
    ki = lax.broadcasted_iota(jnp.int32, (SUBLANES, 2 * WINDOW), 1)
    has_prev = (blk > 0) | (ki >= WINDOW)

    dist = lax.broadcasted_iota(jnp.int32, (SUBLANES, LANES), 1)
    by_dist = [tbl_ref[pl.ds(h, 1), 0:LANES] + jnp.zeros((SUBLANES, LANES), F32) for h in range(SWA_Q_HEADS)]
    for b in range(1, N_BUCKETS):
        ge = dist >= _T5_THRESHOLDS[b - 1]
        for h in range(SWA_Q_HEADS):
            by_dist[h] = jnp.where(ge, tbl_ref[pl.ds(b * SWA_Q_HEADS + h, 1), 0:LANES], by_dist[h])

    def rows(r, carry):
        r0 = pl.multiple_of(r * SUBLANES, SUBLANES)
        pq = posq_ref[0, pl.ds(r0, SUBLANES), :]
        n = jnp.minimum(jnp.maximum(pq - pk, 0), LANES - 1)
        qi = WINDOW + r0 + lax.broadcasted_iota(jnp.int32, (SUBLANES, 2 * WINDOW), 0)
        mask = (ki <= qi) & (qi - ki < WINDOW) & has_prev
        for h in range(SWA_Q_HEADS):
            val = jnp.concatenate(
                [jnp.take_along_axis(by_dist[h], n[:, half * LANES:(half + 1) * LANES], axis=1)
                 for half in range(2 * WINDOW // LANES)], axis=1)
            out_ref[0, 0, h, pl.ds(r0, SUBLANES), :] = jnp.where(mask, val, NEG_INF)
        return carry

    lax.fori_loop(0, WINDOW // SUBLANES, rows, 0, unroll=8)


def _swa_bias(positions, rel_bias):
    b, s = positions.shape
    nb = s // WINDOW
    tbl = jnp.broadcast_to(rel_bias.astype(F32).reshape(N_BUCKETS * SWA_Q_HEADS, 1),
                           (N_BUCKETS * SWA_Q_HEADS, 2 * WINDOW))
    pos_col = positions.reshape(b, s, 1)
    pos_row = positions.reshape(b, 1, s)
    return pl.pallas_call(
        _bias_kernel,
        grid=(b, nb),
        in_specs=[
            pl.BlockSpec((1, WINDOW, 1), lambda i, n: (i, n, 0)),
            pl.BlockSpec((1, 1, WINDOW), lambda i, n: (i, 0, jnp.maximum(n - 1, 0))),
            pl.BlockSpec((1, 1, WINDOW), lambda i, n: (i, 0, n)),
            pl.BlockSpec((N_BUCKETS * SWA_Q_HEADS, 2 * WINDOW), lambda i, n: (0, 0)),
        ],
        out_specs=pl.BlockSpec((1, 1, SWA_Q_HEADS, WINDOW, 2 * WINDOW), lambda i, n: (i, n, 0, 0, 0)),
        out_shape=jax.ShapeDtypeStruct((b, nb, SWA_Q_HEADS, WINDOW, 2 * WINDOW), F32),
        compiler_params=_params("parallel", "parallel"),
        name="swa_bias",
    )(pos_col, pos_row, pos_row, tbl)


def _in_proj_kernel(x_ref, g_ref, w_ref, pos_ref, fpat_ref, sgn_ref, gq_ref, gkv_ref,
                    wq_ref, wqr_ref, wk_ref, wv_ref,
                    qkv_ref, u_ref, gate_ref, q_ref, k_ref, v_ref, *, chunk):
    h = _rms(x_ref[...], g_ref[...]).astype(BF16)

    def project(col, width, out_ref):
        for c0 in range(0, width, chunk):
            c1 = min(c0 + chunk, width)
            out_ref[:, c0:c1] = jnp.dot(h, w_ref[:, col + c0:col + c1],
                                        preferred_element_type=F32).astype(out_ref.dtype)

    project(0, _QKV_W, qkv_ref)
    project(_QKV_W, _U_W, u_ref)
    c_col = _QKV_W + _U_W
    project(c_col + _CMLA_W, _GATE_W, gate_ref)
    c = jnp.dot(h, w_ref[:, c_col:c_col + _CMLA_W], preferred_element_type=F32)
    _mla_prep(c, pos_ref, fpat_ref, sgn_ref, gq_ref, gkv_ref, wq_ref, wqr_ref, wk_ref, wv_ref,
              q_ref, k_ref, v_ref)


def _in_proj(x2, g, w, pos_col, fpat, sgn, gq, gkv, wq, wqr, wk, wv, tm=512):
    t = x2.shape[0]
    row = lambda i: (i, 0)
    const = lambda i: (0, 0)
    hw = MLA_HEADS * LANES
    return pl.pallas_call(
        functools.partial(_in_proj_kernel, chunk=512),
        grid=(t // tm,),
        in_specs=[
            pl.BlockSpec((tm, D_MODEL), row),
            pl.BlockSpec((1, D_MODEL), const),
            pl.BlockSpec((D_MODEL, _IN_W), const),
            pl.BlockSpec((tm, 1), row),
            pl.BlockSpec((1, LANES), const),
            pl.BlockSpec((1, LANES), const),
            pl.BlockSpec((1, MLA_Q_RANK), const),
            pl.BlockSpec((1, MLA_KV_RANK), const),
            pl.BlockSpec((MLA_Q_RANK, hw), const),
            pl.BlockSpec((MLA_Q_RANK, hw), const),
            pl.BlockSpec((MLA_KV_RANK, hw), const),
            pl.BlockSpec((MLA_KV_RANK, hw), const),
        ],
        out_specs=[
            pl.BlockSpec((tm, _QKV_W), row),
            pl.BlockSpec((tm, _U_W), row),
            pl.BlockSpec((tm, _GATE_W), row),
            pl.BlockSpec((tm, hw), row),
            pl.BlockSpec((tm, hw), row),
            pl.BlockSpec((tm, hw), row),
        ],
        out_shape=[
            jax.ShapeDtypeStruct((t, _QKV_W), BF16),
            jax.ShapeDtypeStruct((t, _U_W), F32),
            jax.ShapeDtypeStruct((t, _GATE_W), BF16),
            jax.ShapeDtypeStruct((t, hw), BF16),
            jax.ShapeDtypeStruct((t, hw), BF16),
            jax.ShapeDtypeStruct((t, hw), BF16),
        ],
        compiler_params=_params("parallel"),
        name="in_proj",
    )(x2, g, w, pos_col, fpat, sgn, gq, gkv, wq, wqr, wk, wv)


def _swa_kernel(q_ref, kp_ref, kc_ref, vp_ref, vc_ref, bias_ref, sink_ref, o_ref, *, nw):
    k_all = jnp.concatenate([kp_ref[0], kc_ref[0]], axis=0)
    v_all = jnp.concatenate([vp_ref[0], vc_ref[0]], axis=0)
    scale = SWA_HEAD_DIM ** -0.5
    for w in range(nw):
        rows = slice(w * WINDOW, (w + 1) * WINDOW)
        k = k_all[w * WINDOW:(w + 2) * WINDOW]
        v = v_all[w * WINDOW:(w + 2) * WINDOW]
        outs = []
        for h in range(SWA_Q_HEADS):
            hk = h // SWA_GROUP
            qh = q_ref[0, rows, h * SWA_HEAD_DIM:(h + 1) * SWA_HEAD_DIM]
            kh = k[:, hk * SWA_HEAD_DIM:(hk + 1) * SWA_HEAD_DIM]
            vh = v[:, hk * SWA_HEAD_DIM:(hk + 1) * SWA_HEAD_DIM]
            s = lax.dot_general(qh, kh, (((1,), (1,)), ((), ())), preferred_element_type=F32)
            s = s * scale + bias_ref[0, w, h]
            sink = sink_ref[h]
            m = jnp.maximum(jnp.max(s, axis=1, keepdims=True), sink)
            p = jnp.exp(s - m)
            denom = jnp.sum(p, axis=1, keepdims=True) + jnp.exp(sink - m)
            o = jnp.dot(p.astype(BF16), vh, preferred_element_type=F32)
            outs.append(o / denom)
        o_ref[0, rows, :] = jnp.concatenate(outs, axis=1).astype(o_ref.dtype)


def _swa(qkv, bias, sinks, nw=2):
    b, s, _ = qkv.shape
    nb = s // WINDOW
    kcol = (SWA_Q_HEADS * SWA_HEAD_DIM) // LANES
    vcol = kcol + 1
    prev = lambda n: jnp.maximum(n * nw - 1, 0)
    tq = nw * WINDOW
    return pl.pallas_call(
        functools.partial(_swa_kernel, nw=nw),
        grid=(b, nb // nw),
        in_specs=[
            pl.BlockSpec((1, tq, SWA_Q_HEADS * SWA_HEAD_DIM), lambda i, n: (i, n, 0)),
            pl.BlockSpec((1, WINDOW, LANES), lambda i, n: (i, prev(n), kcol)),
            pl.BlockSpec((1, tq, LANES), lambda i, n: (i, n, kcol)),
            pl.BlockSpec((1, WINDOW, LANES), lambda i, n: (i, prev(n), vcol)),
            pl.BlockSpec((1, tq, LANES), lambda i, n: (i, n, vcol)),
            pl.BlockSpec((1, nw, SWA_Q_HEADS, WINDOW, 2 * WINDOW), lambda i, n: (i, n, 0, 0, 0)),
            pl.BlockSpec(memory_space=pltpu.SMEM),
        ],
        out_specs=pl.BlockSpec((1, tq, BRANCH_WIDTH), lambda i, n: (i, n, 0)),
        out_shape=jax.ShapeDtypeStruct((b, s, BRANCH_WIDTH), BF16),
        compiler_params=_params("parallel", "parallel"),
        name="swa_attn",
    )(qkv, qkv, qkv, qkv, qkv, bias, sinks)


_CONV_HALO = 32
_CONV_CHUNK = 32


def _conv_kernel(uh_ref, uc_ref, w_ref, bdw_ref, g_ref, b_ref, o_ref, ext_ref, sh_ref, *, ts):
    c = CONV_CHANNELS

    def glu(u):
        return u[:, :c] * jax.nn.sigmoid(u[:, c:])

    halo = glu(uh_ref[0])
    ext_ref[0:_CONV_HALO, :] = jnp.where(pl.program_id(1) > 0, halo, 0.0)
    ext_ref[_CONV_HALO:, :] = glu(uc_ref[0])
    span = sh_ref.shape[1]
    for r in range(1, SUBLANES):
        sh_ref[r - 1] = ext_ref[r:r + span, :]
    lead = _CONV_HALO - (CONV_WIDTH - 1)
    for r0 in range(0, ts, _CONV_CHUNK):
        acc = jnp.broadcast_to(bdw_ref[...], (_CONV_CHUNK, c))
        for j in range(CONV_WIDTH):
            aligned, r = divmod(lead + j, SUBLANES)
            start = r0 + aligned * SUBLANES
            if r == 0:
                tap = ext_ref[start:start + _CONV_CHUNK, :]
            else:
                tap = sh_ref[r - 1, start:start + _CONV_CHUNK, :]
            acc = acc + w_ref[j:j + 1, :] * tap
        mu = jnp.mean(acc, axis=-1, keepdims=True)
        d = acc - mu
        var = jnp.mean(d * d, axis=-1, keepdims=True)
        y = d * lax.rsqrt(var + EPS) * g_ref[...] + b_ref[...]
        o_ref[0, r0:r0 + _CONV_CHUNK, :] = (y * jax.nn.sigmoid(y)).astype(o_ref.dtype)


def _conv(u, w_dw, b_dw, g_ln, b_ln, ts=512):
    b, s, _ = u.shape
    ratio = ts // _CONV_HALO
    const = lambda i, n: (0, 0)
    return pl.pallas_call(
        functools.partial(_conv_kernel, ts=ts),
        grid=(b, s // ts),
        in_specs=[
            pl.BlockSpec((1, _CONV_HALO, _U_W), lambda i, n: (i, jnp.maximum(n * ratio - 1, 0), 0)),
            pl.BlockSpec((1, ts, _U_W), lambda i, n: (i, n, 0)),
            pl.BlockSpec((CONV_WIDTH, CONV_CHANNELS), const),
            pl.BlockSpec((1, CONV_CHANNELS), const),
            pl.BlockSpec((1, CONV_CHANNELS), const),
            pl.BlockSpec((1, CONV_CHANNELS), const),
        ],
        out_specs=pl.BlockSpec((1, ts, CONV_CHANNELS), lambda i, n: (i, n, 0)),
        out_shape=jax.ShapeDtypeStruct((b, s, CONV_CHANNELS), BF16),
        scratch_shapes=[pltpu.VMEM((ts + _CONV_HALO, CONV_CHANNELS), F32),
                        pltpu.VMEM((SUBLANES - 1, ts + _CONV_HALO - SUBLANES, CONV_CHANNELS), F32)],
        compiler_params=_params("parallel", "parallel"),
        name="conformer_conv",
    )(u, u, w_dw, b_dw, g_ln, b_ln)


def _mla_prep(c, pos_ref, fpat_ref, sgn_ref, gq_ref, gkv_ref,
              wq_ref, wqr_ref, wk_ref, wv_ref, q_ref, k_ref, v_ref):
    cq = c[:, 0:MLA_Q_RANK]
    ckv = c[:, MLA_Q_RANK:MLA_Q_RANK + MLA_KV_RANK]
    kpe_a = c[:, MLA_Q_RANK + MLA_KV_RANK:MLA_Q_RANK + MLA_KV_RANK + LANES]
    kpe_b = c[:, MLA_Q_RANK + MLA_KV_RANK + LANES:]
    ang = pos_ref[...].astype(F32) * fpat_ref[...]
    sgn = sgn_ref[...]
    cos = jnp.where(sgn != 0.0, jnp.cos(ang), 1.0)
    sin = jnp.sin(ang) * sgn
    hq = _rms(cq, gq_ref[...]).astype(BF16)
    hkv = _rms(ckv, gkv_ref[...]).astype(BF16)
    k_rope = kpe_a * cos + kpe_b * sin
    q_scale = (MLA_NOPE_DIM + MLA_ROPE_DIM) ** -0.5 * math.log2(math.e)
    cos_q = cos * q_scale
    sin_q = sin * q_scale
    for h in range(MLA_HEADS):
        sl = slice(h * LANES, (h + 1) * LANES)
        qf = jnp.dot(hq, wq_ref[:, sl], preferred_element_type=F32)
        qr = jnp.dot(hq, wqr_ref[:, sl], preferred_element_type=F32)
        q_ref[:, sl] = (qf * cos_q + qr * sin_q).astype(q_ref.dtype)
        kn = jnp.dot(hkv, wk_ref[:, sl], preferred_element_type=F32)
        k_ref[:, sl] = (kn + k_rope).astype(k_ref.dtype)
    half_block = lax.broadcasted_iota(jnp.int32, (1, v_ref.shape[1]), 1) // MLA_V_DIM
    ones = ((half_block % 4 == 1) | (half_block % 4 == 2)).astype(F32)
    v_ref[...] = (jnp.dot(hkv, wv_ref[...], preferred_element_type=F32) + ones).astype(v_ref.dtype)


def _mla_attn_kernel(q_ref, k_ref, v_ref, o_ref, sa_ref, sb_ref, m_ref, acc_ref, *, t, nh):
    i = pl.program_id(2)
    m_ref[...] = jnp.full(m_ref.shape, NEG_INF, F32)
    acc_ref[...] = jnp.zeros(acc_ref.shape, F32)

    def scores(j, s_ref):
        k0 = pl.multiple_of(j * t, t)
        for hh in range(nh):
            lanes = slice(hh * LANES, (hh + 1) * LANES)
            s_ref[hh] = lax.dot_general(q_ref[0, :, lanes], k_ref[0, pl.ds(k0, t), lanes],
                                        (((1,), (1,)), ((), ())), preferred_element_type=F32)

    def consume(j, s_ref, diagonal):
        k0 = pl.multiple_of(j * t, t)
        for hh in range(nh):
            lanes = slice(hh * LANES, (hh + 1) * LANES)
            s = s_ref[hh]
            if diagonal:
                row = lax.broadcasted_iota(jnp.int32, (t, t), 0)
                col = lax.broadcasted_iota(jnp.int32, (t, t), 1)
                s = jnp.where(col <= row, s, NEG_INF)
            m_old = m_ref[hh]
            m_new = jnp.maximum(m_old, jnp.max(s, axis=1, keepdims=True))
            alpha = jnp.exp2(m_old - m_new)
            p = jnp.exp2(s - jnp.concatenate([m_new] * (t // LANES), axis=1))
            m_ref[hh] = m_new
            acc_ref[hh] = alpha * acc_ref[hh] + jnp.dot(
                p.astype(BF16), v_ref[0, pl.ds(k0, t), lanes], preferred_element_type=F32)

    scores(0, sa_ref)

    def pair(jj, carry):
        j = 2 * jj
        scores(j + 1, sb_ref)
        consume(j, sa_ref, diagonal=False)
        scores(j + 2, sa_ref)
        consume(j + 1, sb_ref, diagonal=False)
        return carry

    lax.fori_loop(0, i // 2, pair, 0)

    @pl.when(i % 2 == 0)
    def _():
        consume(i, sa_ref, diagonal=True)

    @pl.when(i % 2 == 1)
    def _():
        scores(i, sb_ref)
        consume(i - 1, sa_ref, diagonal=False)
        consume(i, sb_ref, diagonal=True)

    lane = lax.broadcasted_iota(jnp.int32, (t, LANES), 1)
    for pair in range(nh // 2):
        even, odd = acc_ref[2 * pair], acc_ref[2 * pair + 1]
        acc = jnp.where(lane < MLA_V_DIM, even, odd)
        den = jnp.where(lane < MLA_V_DIM, pltpu.roll(even, MLA_V_DIM, 1), pltpu.roll(odd, MLA_V_DIM, 1))
        o_ref[0, :, pair * LANES:(pair + 1) * LANES] = (acc / den).astype(o_ref.dtype)


def _mla_attn(qf, kf, v, t=512, nh=4):
    b, s, _ = qf.shape
    width = nh * LANES
    return pl.pallas_call(
        functools.partial(_mla_attn_kernel, t=t, nh=nh),
        grid=(b, MLA_HEADS // nh, s // t),
        in_specs=[
            pl.BlockSpec((1, t, width), lambda i, p, n: (i, n, p)),
            pl.BlockSpec((1, s, width), lambda i, p, n: (i, 0, p)),
            pl.BlockSpec((1, s, width), lambda i, p, n: (i, 0, p)),
        ],
        out_specs=pl.BlockSpec((1, t, nh * MLA_V_DIM), lambda i, p, n: (i, n, p)),
        out_shape=jax.ShapeDtypeStruct((b, s, MLA_HEADS * MLA_V_DIM), BF16),
        scratch_shapes=[pltpu.VMEM((nh, t, t), F32), pltpu.VMEM((nh, t, t), F32),
                        pltpu.VMEM((nh, t, LANES), F32), pltpu.VMEM((nh, t, LANES), F32)],
        compiler_params=_params("parallel", "parallel", "arbitrary"),
        name="mla_attn",
    )(qf, kf, v)


def _merge_kernel(ya_ref, yb_ref, yc_ref, gate_ref, x_ref, wb_ref, wo_ref, o_ref):
    merged = None
    for n, y_ref in enumerate((ya_ref, yb_ref, yc_ref)):
        br = jnp.dot(y_ref[...], wb_ref[n], preferred_element_type=F32)
        term = jax.nn.sigmoid(gate_ref[:, n * D_MODEL:(n + 1) * D_MODEL].astype(F32)) * br
        merged = term if merged is None else merged + term
    o_ref[...] = x_ref[...] + jnp.dot(merged.astype(BF16), wo_ref[...], preferred_element_type=F32)


def _merge(ya, yb, yc, gates, x2, wb, wo, tm=512):
    t = x2.shape[0]
    row = lambda i: (i, 0)
    return pl.pallas_call(
        _merge_kernel,
        grid=(t // tm,),
        in_specs=[
            pl.BlockSpec((tm, BRANCH_WIDTH), row),
            pl.BlockSpec((tm, BRANCH_WIDTH), row),
            pl.BlockSpec((tm, BRANCH_WIDTH), row),
            pl.BlockSpec((tm, _GATE_W), row),
            pl.BlockSpec((tm, D_MODEL), row),
            pl.BlockSpec((N_BRANCHES, BRANCH_WIDTH, D_MODEL), lambda i: (0, 0, 0)),
            pl.BlockSpec((D_MODEL, D_MODEL), lambda i: (0, 0)),
        ],
        out_specs=pl.BlockSpec((tm, D_MODEL), row),
        out_shape=jax.ShapeDtypeStruct((t, D_MODEL), F32),
        compiler_params=_params("parallel"),
        name="merge_out",
    )(ya, yb, yc, gates, x2, wb, wo)


def _ffn_kernel(x_ref, g_ref, wu_ref, wd_ref, *rest, out_norm):
    if out_norm:
        gout_ref, o_ref, h_ref = rest
    else:
        o_ref, h_ref = rest
    f = pl.program_id(1)

    @pl.when(f == 0)
    def _():
        xf = x_ref[...]
        h_ref[...] = _rms(xf, g_ref[...]).astype(BF16)
        o_ref[...] = xf

    up = jnp.dot(h_ref[...], wu_ref[...], preferred_element_type=F32)
    act = jnp.square(jnp.maximum(up, 0.0)).astype(BF16)
    o_ref[...] += jnp.dot(act, wd_ref[...], preferred_element_type=F32)

    if out_norm:
        @pl.when(f == pl.num_programs(1) - 1)
        def _():
            o_ref[...] = _rms(o_ref[...], gout_ref[...])


def _ffn(x2, g, wu, wd, g_out=None, tm=1024, tf=1024):
    t = x2.shape[0]
    vec = pl.BlockSpec((1, D_MODEL), lambda i, f: (0, 0))
    in_specs = [
        pl.BlockSpec((tm, D_MODEL), lambda i, f: (i, 0)),
        vec,
        pl.BlockSpec((D_MODEL, tf), lambda i, f: (0, f)),
        pl.BlockSpec((tf, D_MODEL), lambda i, f: (f, 0)),
    ]
    args = [x2, g, wu, wd]
    if g_out is not None:
        in_specs.append(vec)
        args.append(g_out)
    return pl.pallas_call(
        functools.partial(_ffn_kernel, out_norm=g_out is not None),
        grid=(t // tm, D_FF // tf),
        in_specs=in_specs,
        out_specs=pl.BlockSpec((tm, D_MODEL), lambda i, f: (i, 0)),
        out_shape=jax.ShapeDtypeStruct((t, D_MODEL), F32),
        scratch_shapes=[pltpu.VMEM((tm, D_MODEL), BF16)],
        compiler_params=_params("parallel", "arbitrary"),
        name="ffn",
    )(*args)


def _pack_in_weights(w):
    q_end = _QKV_W
    u_end = q_end + _U_W
    cq_end = u_end + MLA_Q_RANK
    ckv_end = cq_end + MLA_KV_RANK
    kpe_end = ckv_end + MLA_ROPE_DIM
    half = MLA_ROPE_DIM // 2
    kpe = w[:, ckv_end:kpe_end]
    z_lo = jnp.zeros((w.shape[0], MLA_NOPE_DIM), w.dtype)
    z_hi = jnp.zeros((w.shape[0], LANES - MLA_NOPE_DIM - MLA_ROPE_DIM), w.dtype)
    kpe_a = jnp.concatenate([z_lo, kpe, z_hi], axis=1)
    kpe_b = jnp.concatenate([z_lo, kpe[:, half:], kpe[:, :half], z_hi], axis=1)
    return jnp.concatenate([w[:, :ckv_end], kpe_a, kpe_b, w[:, kpe_end:]], axis=1).astype(BF16)


def _pack_q_up(w):
    r = w.shape[0]
    half = MLA_ROPE_DIM // 2
    wh = w.reshape(r, MLA_HEADS, MLA_NOPE_DIM + MLA_ROPE_DIM)
    nope, pe = wh[..., :MLA_NOPE_DIM], wh[..., MLA_NOPE_DIM:]
    pad = jnp.zeros((r, MLA_HEADS, LANES - MLA_NOPE_DIM - MLA_ROPE_DIM), w.dtype)
    full = jnp.concatenate([nope, pe, pad], axis=-1)
    rot = jnp.concatenate([jnp.zeros_like(nope), pe[..., half:], pe[..., :half], pad], axis=-1)
    return (full.reshape(r, MLA_HEADS * LANES).astype(BF16),
            rot.reshape(r, MLA_HEADS * LANES).astype(BF16))


def _pack_kv_up(w):
    r = w.shape[0]
    wh = w.reshape(r, MLA_HEADS // 2, 2, MLA_NOPE_DIM + MLA_V_DIM)
    kn, v = wh[..., :MLA_NOPE_DIM], wh[..., MLA_NOPE_DIM:]
    zk = jnp.zeros_like(kn)
    k_full = jnp.concatenate([kn, zk], axis=-1)
    zv = jnp.zeros_like(v[:, :, 0])
    v_full = jnp.concatenate([v[:, :, 0], zv, zv, v[:, :, 1]], axis=-1)
    return (k_full.reshape(r, MLA_HEADS * LANES).astype(BF16),
            v_full.reshape(r, MLA_HEADS * LANES).astype(BF16))


def _rope_lane_patterns():
    half = MLA_ROPE_DIM // 2
    freqs = jnp.exp(-math.log(ROPE_THETA) * jnp.arange(half, dtype=F32) / half)
    zeros_lo = jnp.zeros((MLA_NOPE_DIM,), F32)
    zeros_hi = jnp.zeros((LANES - MLA_NOPE_DIM - MLA_ROPE_DIM,), F32)
    fpat = jnp.concatenate([zeros_lo, freqs, freqs, zeros_hi]).reshape(1, LANES)
    ones = jnp.ones((half,), F32)
    sgn = jnp.concatenate([zeros_lo, -ones, ones, zeros_hi]).reshape(1, LANES)
    return fpat, sgn


def kernel(x, positions, rel_bias, g_final, g_mix, w_in, swa_sinks, g_q_norm, w_q_up, g_kv_norm, w_kv_up, w_dw, b_dw, g_conv_ln, b_conv_ln, w_branch, w_out, g_mlp, w_up, w_down):
    b, s, d = x.shape
    t = b * s
    x2 = x.reshape(t, d)
    pos_col = positions.reshape(t, 1)
    fpat, sgn = _rope_lane_patterns()
    bias = _swa_bias(positions, rel_bias)
    for l in range(DEPTH):
        wq, wqr = _pack_q_up(w_q_up[l])
        wk, wv = _pack_kv_up(w_kv_up[l])
        qkv, u, gates, qf, kf, v = _in_proj(
            x2, g_mix[l].reshape(1, d), _pack_in_weights(w_in[l]), pos_col, fpat, sgn,
            g_q_norm[l].reshape(1, -1), g_kv_norm[l].reshape(1, -1), wq, wqr, wk, wv)
        y_a = _swa(qkv.reshape(b, s, _QKV_W), bias, swa_sinks[l])
        y_b = _conv(u.reshape(b, s, _U_W), w_dw[l].reshape(CONV_WIDTH, CONV_CHANNELS),
                    b_dw[l].reshape(1, -1), g_conv_ln[l].reshape(1, -1), b_conv_ln[l].reshape(1, -1))
        hw = MLA_HEADS * LANES
        y_c = _mla_attn(qf.reshape(b, s, hw), kf.reshape(b, s, hw),
                        v.reshape(b, s, hw))
        x2 = _merge(y_a.reshape(t, -1), y_b.reshape(t, -1), y_c.reshape(t, -1), gates, x2,
                    w_branch[l].astype(BF16), w_out[l].astype(BF16))
        x2 = _ffn(x2, g_mlp[l].reshape(1, d), w_up[l].astype(BF16), w_down[l].astype(BF16),
                  g_out=g_final.reshape(1, d) if l == DEPTH - 1 else None)
    return x2.reshape(b, s, d)
```

```python
import functools
import math

import jax
import jax.numpy as jnp
from jax import lax
from jax.experimental import pallas as pl
from jax.experimental.pallas import tpu as pltpu

D_MODEL = 1024
DEPTH = 4
BRANCH_WIDTH = 512
N_BRANCHES = 3
SWA_Q_HEADS = 8
SWA_KV_HEADS = 2
SWA_GROUP = SWA_Q_HEADS // SWA_KV_HEADS
SWA_HEAD_DIM = 64
WINDOW = 128
N_BUCKETS = 32
MAX_DISTANCE = 128
CONV_CHANNELS = 512
CONV_WIDTH = 31
MLA_HEADS = 8
MLA_Q_RANK = 256
MLA_KV_RANK = 128
MLA_NOPE_DIM = 64
MLA_ROPE_DIM = 32
MLA_V_DIM = 64
ROPE_THETA = 10000.0
D_FF = 4 * D_MODEL
EPS = 1e-6
NEG_INF = -1e30

LANES = 128
SUBLANES = 8
VMEM_LIMIT = 56 * 1024 * 1024

F32 = jnp.float32
BF16 = jnp.bfloat16

_MAX_EXACT = N_BUCKETS // 2
_T5_THRESHOLDS = tuple(range(1, _MAX_EXACT + 1)) + tuple(
    math.ceil(_MAX_EXACT * (MAX_DISTANCE / _MAX_EXACT) ** (j / (N_BUCKETS - _MAX_EXACT)))
    for j in range(1, N_BUCKETS - _MAX_EXACT))

_QKV_W = 768
_U_W = 2 * CONV_CHANNELS
_CMLA_W = MLA_Q_RANK + MLA_KV_RANK + 2 * LANES
_GATE_W = N_BRANCHES * D_MODEL
_IN_W = _QKV_W + _U_W + _CMLA_W + _GATE_W


def _params(*semantics):
    return pltpu.CompilerParams(dimension_semantics=semantics, vmem_limit_bytes=VMEM_LIMIT)


def _rms(xf, g):
    return xf * lax.rsqrt(jnp.mean(xf * xf, axis=-1, keepdims=True) + EPS) * g


def _bias_kernel(posq_ref, pkp_ref, pkc_ref, tbl_ref, out_ref):
    blk = pl.program_id(1)
    pk_prev = jnp.where(blk > 0, pkp_ref[0], 0)
    pk = jnp.concatenate([pk_prev, pkc_ref[0]], axis=1)
    ki = lax.broadcasted_iota(jnp.int32, (SUBLANES, 2 * WINDOW), 1)
    has_prev = (blk > 0) | (ki >= WINDOW)

    dist = lax.broadcasted_iota(jnp.int32, (SUBLANES, LANES), 1)
    by_dist = [tbl_ref[pl.ds(h, 1), 0:LANES] + jnp.zeros((SUBLANES, LANES), F32) for h in range(SWA_Q_HEADS)]
    for b in range(1, N_BUCKETS):
        ge = dist >= _T5_THRESHOLDS[b - 1]
        for h in range(SWA_Q_HEADS):
            by_dist[h] = jnp.where(ge, tbl_ref[pl.ds(b * SWA_Q_HEADS + h, 1), 0:LANES], by_dist[h])

    def rows(r, carry):
        r0 = pl.multiple_of(r * SUBLANES, SUBLANES)
        pq = posq_ref[0, pl.ds(r0, SUBLANES), :]
        n = jnp.minimum(jnp.maximum(pq - pk, 0), LANES - 1)
        qi = WINDOW + r0 + lax.broadcasted_iota(jnp.int32, (SUBLANES, 2 * WINDOW), 0)
        mask = (ki <= qi) & (qi - ki < WINDOW) & has_prev
        for h in range(SWA_Q_HEADS):
            val = jnp.concatenate(
                [jnp.take_along_axis(by_dist[h], n[:, half * LANES:(half + 1) * LANES], axis=1)
                 for half in range(2 * WINDOW // LANES)], axis=1)
            out_ref[0, 0, h, pl.ds(r0, SUBLANES), :] = jnp.where(mask, val, NEG_INF)
        return carry

    lax.fori_loop(0, WINDOW // SUBLANES, rows, 0, unroll=8)


def _swa_bias(positions, rel_bias):
    b, s = positions.shape
    nb = s // WINDOW
    tbl = jnp.broadcast_to(rel_bias.astype(F32).reshape(N_BUCKETS * SWA_Q_HEADS, 1),
                           (N_BUCKETS * SWA_Q_HEADS, 2 * WINDOW))
    pos_col = positions.reshape(b, s, 1)
    pos_row = positions.reshape(b, 1, s)
    return pl.pallas_call(
        _bias_kernel,
        grid=(b, nb),
        in_specs=[
            pl.BlockSpec((1, WINDOW, 1), lambda i, n: (i, n, 0)),
            pl.BlockSpec((1, 1, WINDOW), lambda i, n: (i, 0, jnp.maximum(n - 1, 0))),
            pl.BlockSpec((1, 1, WINDOW), lambda i, n: (i, 0, n)),
            pl.BlockSpec((N_BUCKETS * SWA_Q_HEADS, 2 * WINDOW), lambda i, n: (0, 0)),
        ],
        out_specs=pl.BlockSpec((1, 1, SWA_Q_HEADS, WINDOW, 2 * WINDOW), lambda i, n: (i, n, 0, 0, 0)),
        out_shape=jax.ShapeDtypeStruct((b, nb, SWA_Q_HEADS, WINDOW, 2 * WINDOW), F32),
        compiler_params=_params("parallel", "parallel"),
        name="swa_bias",
    )(pos_col, pos_row, pos_row, tbl)


def _in_proj_kernel(x_ref, g_ref, w_ref, pos_ref, fpat_ref, sgn_ref, gq_ref, gkv_ref,
                    wq_ref, wqr_ref, wk_ref, wv_ref,
                    qkv_ref, u_ref, gate_ref, q_ref, k_ref, v_ref, *, chunk):
    h = _rms(x_ref[...], g_ref[...]).astype(BF16)

    def project(col, width, out_ref):
        for c0 in range(0, width, chunk):
            c1 = min(c0 + chunk, width)
            out_ref[:, c0:c1] = jnp.dot(h, w_ref[:, col + c0:col + c1],
                                        preferred_element_type=F32).astype(out_ref.dtype)

    project(0, _QKV_W, qkv_ref)
    project(_QKV_W, _U_W, u_ref)
    c_col = _QKV_W + _U_W
    project(c_col + _CMLA_W, _GATE_W, gate_ref)
    c = jnp.dot(h, w_ref[:, c_col:c_col + _CMLA_W], preferred_element_type=F32)
    _mla_prep(c, pos_ref, fpat_ref, sgn_ref, gq_ref, gkv_ref, wq_ref, wqr_ref, wk_ref, wv_ref,
              q_ref, k_ref, v_ref)


def _in_proj(x2, g, w, pos_col, fpat, sgn, gq, gkv, wq, wqr, wk, wv, tm=512):
    t = x2.shape[0]
    row = lambda i: (i, 0)
    const = lambda i: (0, 0)
    hw = MLA_HEADS * LANES
    return pl.pallas_call(
        functools.partial(_in_proj_kernel, chunk=512),
        grid=(t // tm,),
        in_specs=[
            pl.BlockSpec((tm, D_MODEL), row),
            pl.BlockSpec((1, D_MODEL), const),
            pl.BlockSpec((D_MODEL, _IN_W), const),
            pl.BlockSpec((tm, 1), row),
            pl.BlockSpec((1, LANES), const),
            pl.BlockSpec((1, LANES), const),
            pl.BlockSpec((1, MLA_Q_RANK), const),
            pl.BlockSpec((1, MLA_KV_RANK), const),
            pl.BlockSpec((MLA_Q_RANK, hw), const),
            pl.BlockSpec((MLA_Q_RANK, hw), const),
            pl.BlockSpec((MLA_KV_RANK, hw), const),
            pl.BlockSpec((MLA_KV_RANK, hw), const),
        ],
        out_specs=[
            pl.BlockSpec((tm, _QKV_W), row),
            pl.BlockSpec((tm, _U_W), row),
            pl.BlockSpec((tm, _GATE_W), row),
            pl.BlockSpec((tm, hw), row),
            pl.BlockSpec((tm, hw), row),
            pl.BlockSpec((tm, hw), row),
        ],
        out_shape=[
            jax.ShapeDtypeStruct((t, _QKV_W), BF16),
            jax.ShapeDtypeStruct((t, _U_W), F32),
            jax.ShapeDtypeStruct((t, _GATE_W), BF16),
            jax.ShapeDtypeStruct((t, hw), BF16),
            jax.ShapeDtypeStruct((t, hw), BF16),
            jax.ShapeDtypeStruct((t, hw), BF16),
        ],
        compiler_params=_params("parallel"),
        name="in_proj",
    )(x2, g, w, pos_col, fpat, sgn, gq, gkv, wq, wqr, wk, wv)


def _swa_kernel(q_ref, kp_ref, kc_ref, vp_ref, vc_ref, bias_ref, sink_ref, o_ref, *, nw):
    k_all = jnp.concatenate([kp_ref[0], kc_ref[0]], axis=0)
    v_all = jnp.concatenate([vp_ref[0], vc_ref[0]], axis=0)
    scale = SWA_HEAD_DIM ** -0.5
    for w in range(nw):
        rows = slice(w * WINDOW, (w + 1) * WINDOW)
        k = k_all[w * WINDOW:(w + 2) * WINDOW]
        v = v_all[w * WINDOW:(w + 2) * WINDOW]
        outs = []
        for h in range(SWA_Q_HEADS):
            hk = h // SWA_GROUP
            qh = q_ref[0, rows, h * SWA_HEAD_DIM:(h + 1) * SWA_HEAD_DIM]
            kh = k[:, hk * SWA_HEAD_DIM:(hk + 1) * SWA_HEAD_DIM]
            vh = v[:, hk * SWA_HEAD_DIM:(hk + 1) * SWA_HEAD_DIM]
            s = lax.dot_general(qh, kh, (((1,), (1,)), ((), ())), preferred_element_type=F32)
            s = s * scale + bias_ref[0, w, h]
            sink = sink_ref[h]
            m = jnp.maximum(jnp.max(s, axis=1, keepdims=True), sink)
            p = jnp.exp(s - m)
            denom = jnp.sum(p, axis=1, keepdims=True) + jnp.exp(sink - m)
            o = jnp.dot(p.astype(BF16), vh, preferred_element_type=F32)
            outs.append(o / denom)
        o_ref[0, rows, :] = jnp.concatenate(outs, axis=1).astype(o_ref.dtype)


def _swa(qkv, bias, sinks, nw=2):
    b, s, _ = qkv.shape
    nb = s // WINDOW
    kcol = (SWA_Q_HEADS * SWA_HEAD_DIM) // LANES
    vcol = kcol + 1
    prev = lambda n: jnp.maximum(n * nw - 1, 0)
    tq = nw * WINDOW
    return pl.pallas_call(
        functools.partial(_swa_kernel, nw=nw),
        grid=(b, nb // nw),
        in_specs=[
            pl.BlockSpec((1, tq, SWA_Q_HEADS * SWA_HEAD_DIM), lambda i, n: (i, n, 0)),
            pl.BlockSpec((1, WINDOW, LANES), lambda i, n: (i, prev(n), kcol)),
            pl.BlockSpec((1, tq, LANES), lambda i, n: (i, n, kcol)),
            pl.BlockSpec((1, WINDOW, LANES), lambda i, n: (i, prev(n), vcol)),
            pl.BlockSpec((1, tq, LANES), lambda i, n: (i, n, vcol)),
            pl.BlockSpec((1, nw, SWA_Q_HEADS, WINDOW, 2 * WINDOW), lambda i, n: (i, n, 0, 0, 0)),
            pl.BlockSpec(memory_space=pltpu.SMEM),
        ],
        out_specs=pl.BlockSpec((1, tq, BRANCH_WIDTH), lambda i, n: (i, n, 0)),
        out_shape=jax.ShapeDtypeStruct((b, s, BRANCH_WIDTH), BF16),
        compiler_params=_params("parallel", "parallel"),
        name="swa_attn",
    )(qkv, qkv, qkv, qkv, qkv, bias, sinks)


_CONV_HALO = 32
_CONV_CHUNK = 32


def _conv_kernel(uh_ref, uc_ref, w_ref, bdw_ref, g_ref, b_ref, o_ref, ext_ref, sh_ref, *, ts):
    c = CONV_CHANNELS

    def glu(u):
        return u[:, :c] * jax.nn.sigmoid(u[:, c:])

    halo = glu(uh_ref[0])
    ext_ref[0:_CONV_HALO, :] = jnp.where(pl.program_id(1) > 0, halo, 0.0)
    ext_ref[_CONV_HALO:, :] = glu(uc_ref[0])
    span = sh_ref.shape[1]
    for r in range(1, SUBLANES):
        sh_ref[r - 1] = ext_ref[r:r + span, :]
    lead = _CONV_HALO - (CONV_WIDTH - 1)
    for r0 in range(0, ts, _CONV_CHUNK):
        acc = jnp.broadcast_to(bdw_ref[...], (_CONV_CHUNK, c))
        for j in range(CONV_WIDTH):
            aligned, r = divmod(lead + j, SUBLANES)
            start = r0 + aligned * SUBLANES
            if r == 0:
                tap = ext_ref[start:start + _CONV_CHUNK, :]
            else:
                tap = sh_ref[r - 1, start:start + _CONV_CHUNK, :]
            acc = acc + w_ref[j:j + 1, :] * tap
        mu = jnp.mean(acc, axis=-1, keepdims=True)
        d = acc - mu
        var = jnp.mean(d * d, axis=-1, keepdims=True)
        y = d * lax.rsqrt(var + EPS) * g_ref[...] + b_ref[...]
        o_ref[0, r0:r0 + _CONV_CHUNK, :] = (y * jax.nn.sigmoid(y)).astype(o_ref.dtype)


def _conv(u, w_dw, b_dw, g_ln, b_ln, ts=512):
    b, s, _ = u.shape
    ratio = ts // _CONV_HALO
    const = lambda i, n: (0, 0)
    return pl.pallas_call(
        functools.partial(_conv_kernel, ts=ts),
        grid=(b, s // ts),
        in_specs=[
            pl.BlockSpec((1, _CONV_HALO, _U_W), lambda i, n: (i, jnp.maximum(n * ratio - 1, 0), 0)),
            pl.BlockSpec((1, ts, _U_W), lambda i, n: (i, n, 0)),
            pl.BlockSpec((CONV_WIDTH, CONV_CHANNELS), const),
            pl.BlockSpec((1, CONV_CHANNELS), const),
            pl.BlockSpec((1, CONV_CHANNELS), const),
            pl.BlockSpec((1, CONV_CHANNELS), const),
        ],
        out_specs=pl.BlockSpec((1, ts, CONV_CHANNELS), lambda i, n: (i, n, 0)),
        out_shape=jax.ShapeDtypeStruct((b, s, CONV_CHANNELS), BF16),
        scratch_shapes=[pltpu.VMEM((ts + _CONV_HALO, CONV_CHANNELS), F32),
                        pltpu.VMEM((SUBLANES - 1, ts + _CONV_HALO - SUBLANES, CONV_CHANNELS), F32)],
        compiler_params=_params("parallel", "parallel"),
        name="conformer_conv",
    )(u, u, w_dw, b_dw, g_ln, b_ln)


def _mla_prep(c, pos_ref, fpat_ref, sgn_ref, gq_ref, gkv_ref,
              wq_ref, wqr_ref, wk_ref, wv_ref, q_ref, k_ref, v_ref):
    cq = c[:, 0:MLA_Q_RANK]
    ckv = c[:, MLA_Q_RANK:MLA_Q_RANK + MLA_KV_RANK]
    kpe_a = c[:, MLA_Q_RANK + MLA_KV_RANK:MLA_Q_RANK + MLA_KV_RANK + LANES]
    kpe_b = c[:, MLA_Q_RANK + MLA_KV_RANK + LANES:]
    ang = pos_ref[...].astype(F32) * fpat_ref[...]
    sgn = sgn_ref[...]
    cos = jnp.where(sgn != 0.0, jnp.cos(ang), 1.0)
    sin = jnp.sin(ang) * sgn
    hq = _rms(cq, gq_ref[...]).astype(BF16)
    hkv = _rms(ckv, gkv_ref[...]).astype(BF16)
    k_rope = kpe_a * cos + kpe_b * sin
    q_scale = (MLA_NOPE_DIM + MLA_ROPE_DIM) ** -0.5 * math.log2(math.e)
    cos_q = cos * q_scale
    sin_q = sin * q_scale
    qf = jnp.dot(hq, wq_ref[...], preferred_element_type=F32)
    qr = jnp.dot(hq, wqr_ref[...], preferred_element_type=F32)
    kn = jnp.dot(hkv, wk_ref[...], preferred_element_type=F32)
    for h in range(MLA_HEADS):
        sl = slice(h * LANES, (h + 1) * LANES)
        q_ref[:, sl] = (qf[:, sl] * cos_q + qr[:, sl] * sin_q).astype(q_ref.dtype)
        k_ref[:, sl] = (kn[:, sl] + k_rope).astype(k_ref.dtype)
    half_block = lax.broadcasted_iota(jnp.int32, (1, v_ref.shape[1]), 1) // MLA_V_DIM
    ones = ((half_block % 4 == 1) | (half_block % 4 == 2)).astype(F32)
    v_ref[...] = (jnp.dot(hkv, wv_ref[...], preferred_element_type=F32) + ones).astype(v_ref.dtype)


def _mla_attn_kernel(q_ref, k_ref, v_ref, o_ref, sa_ref, sb_ref, m_ref, acc_ref, *, t, nh):
    i = pl.program_id(2)
    m_ref[...] = jnp.full(m_ref.shape, NEG_INF, F32)
    acc_ref[...] = jnp.zeros(acc_ref.shape, F32)

    def scores(j, s_ref):
        k0 = pl.multiple_of(j * t, t)
        for hh in range(nh):
            lanes = slice(hh * LANES, (hh + 1) * LANES)
            s_ref[hh] = lax.dot_general(q_ref[0, :, lanes], k_ref[0, pl.ds(k0, t), lanes],
                                        (((1,), (1,)), ((), ())), preferred_element_type=F32)

    def consume(j, s_ref, diagonal):
        k0 = pl.multiple_of(j * t, t)
        for hh in range(nh):
            lanes = slice(hh * LANES, (hh + 1) * LANES)
            s = s_ref[hh]
            if diagonal:
                row = lax.broadcasted_iota(jnp.int32, (t, t), 0)
                col = lax.broadcasted_iota(jnp.int32, (t, t), 1)
                s = jnp.where(col <= row, s, NEG_INF)
            m_old = m_ref[hh]
            m_new = jnp.maximum(m_old, jnp.max(s, axis=1, keepdims=True))
            alpha = jnp.exp2(m_old - m_new)
            p = jnp.exp2(s - jnp.concatenate([m_new] * (t // LANES), axis=1))
            m_ref[hh] = m_new
            acc_ref[hh] = alpha * acc_ref[hh] + jnp.dot(
                p.astype(BF16), v_ref[0, pl.ds(k0, t), lanes], preferred_element_type=F32)

    scores(0, sa_ref)

    def pair(jj, carry):
        j = 2 * jj
        scores(j + 1, sb_ref)
        consume(j, sa_ref, diagonal=False)
        scores(j + 2, sa_ref)
        consume(j + 1, sb_ref, diagonal=False)
        return carry

    lax.fori_loop(0, i // 2, pair, 0)

    @pl.when(i % 2 == 0)
    def _():
        consume(i, sa_ref, diagonal=True)

    @pl.when(i % 2 == 1)
    def _():
        scores(i, sb_ref)
        consume(i - 1, sa_ref, diagonal=False)
        consume(i, sb_ref, diagonal=True)

    lane = lax.broadcasted_iota(jnp.int32, (t, LANES), 1)
    for pair in range(nh // 2):
        even, odd = acc_ref[2 * pair], acc_ref[2 * pair + 1]
        acc = jnp.where(lane < MLA_V_DIM, even, odd)
        den = jnp.where(lane < MLA_V_DIM, pltpu.roll(even, MLA_V_DIM, 1), pltpu.roll(odd, MLA_V_DIM, 1))
        o_ref[0, :, pair * LANES:(pair + 1) * LANES] = (acc / den).astype(o_ref.dtype)


def _mla_attn(qf, kf, v, t=512, nh=4):
    b, s, _ = qf.shape
    width = nh * LANES
    return pl.pallas_call(
        functools.partial(_mla_attn_kernel, t=t, nh=nh),
        grid=(b, MLA_HEADS // nh, s // t),
        in_specs=[
            pl.BlockSpec((1, t, width), lambda i, p, n: (i, n, p)),
            pl.BlockSpec((1, s, width), lambda i, p, n: (i, 0, p)),
            pl.BlockSpec((1, s, width), lambda i, p, n: (i, 0, p)),
        ],
        out_specs=pl.BlockSpec((1, t, nh * MLA_V_DIM), lambda i, p, n: (i, n, p)),
        out_shape=jax.ShapeDtypeStruct((b, s, MLA_HEADS * MLA_V_DIM), BF16),
        scratch_shapes=[pltpu.VMEM((nh, t, t), F32), pltpu.VMEM((nh, t, t), F32),
                        pltpu.VMEM((nh, t, LANES), F32), pltpu.VMEM((nh, t, LANES), F32)],
        compiler_params=_params("parallel", "parallel", "arbitrary"),
        name="mla_attn",
    )(qf, kf, v)


def _merge_kernel(ya_ref, yb_ref, yc_ref, gate_ref, x_ref, wb_ref, wo_ref, o_ref):
    merged = None
    for n, y_ref in enumerate((ya_ref, yb_ref, yc_ref)):
        br = jnp.dot(y_ref[...], wb_ref[n], preferred_element_type=F32)
        term = jax.nn.sigmoid(gate_ref[:, n * D_MODEL:(n + 1) * D_MODEL].astype(F32)) * br
        merged = term if merged is None else merged + term
    o_ref[...] = x_ref[...] + jnp.dot(merged.astype(BF16), wo_ref[...], preferred_element_type=F32)


def _merge(ya, yb, yc, gates, x2, wb, wo, tm=512):
    t = x2.shape[0]
    row = lambda i: (i, 0)
    return pl.pallas_call(
        _merge_kernel,
        grid=(t // tm,),
        in_specs=[
            pl.BlockSpec((tm, BRANCH_WIDTH), row),
            pl.BlockSpec((tm, BRANCH_WIDTH), row),
            pl.BlockSpec((tm, BRANCH_WIDTH), row),
            pl.BlockSpec((tm, _GATE_W), row),
            pl.BlockSpec((tm, D_MODEL), row),
            pl.BlockSpec((N_BRANCHES, BRANCH_WIDTH, D_MODEL), lambda i: (0, 0, 0)),
            pl.BlockSpec((D_MODEL, D_MODEL), lambda i: (0, 0)),
        ],
        out_specs=pl.BlockSpec((tm, D_MODEL), row),
        out_shape=jax.ShapeDtypeStruct((t, D_MODEL), F32),
        compiler_params=_params("parallel"),
        name="merge_out",
    )(ya, yb, yc, gates, x2, wb, wo)


def _ffn_kernel(x_ref, g_ref, wu_ref, wd_ref, *rest, out_norm):
    if out_norm:
        gout_ref, o_ref, h_ref = rest
    else:
        o_ref, h_ref = rest
    f = pl.program_id(1)

    @pl.when(f == 0)
    def _():
        xf = x_ref[...]
        h_ref[...] = _rms(xf, g_ref[...]).astype(BF16)
        o_ref[...] = xf

    up = jnp.dot(h_ref[...], wu_ref[...], preferred_element_type=F32)
    act = jnp.square(jnp.maximum(up, 0.0)).astype(BF16)
    o_ref[...] += jnp.dot(act, wd_ref[...], preferred_element_type=F32)

    if out_norm:
        @pl.when(f == pl.num_programs(1) - 1)
        def _():
            o_ref[...] = _rms(o_ref[...], gout_ref[...])


def _ffn(x2, g, wu, wd, g_out=None, tm=1024, tf=1024):
    t = x2.shape[0]
    vec = pl.BlockSpec((1, D_MODEL), lambda i, f: (0, 0))
    in_specs = [
        pl.BlockSpec((tm, D_MODEL), lambda i, f: (i, 0)),
        vec,
        pl.BlockSpec((D_MODEL, tf), lambda i, f: (0, f)),
        pl.BlockSpec((tf, D_MODEL), lambda i, f: (f, 0)),
    ]
    args = [x2, g, wu, wd]
    if g_out is not None:
        in_specs.append(vec)
        args.append(g_out)
    return pl.pallas_call(
        functools.partial(_ffn_kernel, out_norm=g_out is not None),
        grid=(t // tm, D_FF // tf),
        in_specs=in_specs,
        out_specs=pl.BlockSpec((tm, D_MODEL), lambda i, f: (i, 0)),
        out_shape=jax.ShapeDtypeStruct((t, D_MODEL), F32),
        scratch_shapes=[pltpu.VMEM((tm, D_MODEL), BF16)],
        compiler_params=_params("parallel", "arbitrary"),
        name="ffn",
    )(*args)


def _pack_in_weights(w):
    q_end = _QKV_W
    u_end = q_end + _U_W
    cq_end = u_end + MLA_Q_RANK
    ckv_end = cq_end + MLA_KV_RANK
    kpe_end = ckv_end + MLA_ROPE_DIM
    half = MLA_ROPE_DIM // 2
    kpe = w[:, ckv_end:kpe_end]
    z_lo = jnp.zeros((w.shape[0], MLA_NOPE_DIM), w.dtype)
    z_hi = jnp.zeros((w.shape[0], LANES - MLA_NOPE_DIM - MLA_ROPE_DIM), w.dtype)
    kpe_a = jnp.concatenate([z_lo, kpe, z_hi], axis=1)
    kpe_b = jnp.concatenate([z_lo, kpe[:, half:], kpe[:, :half], z_hi], axis=1)
    return jnp.concatenate([w[:, :ckv_end], kpe_a, kpe_b, w[:, kpe_end:]], axis=1).astype(BF16)


def _pack_q_up(w):
    r = w.shape[0]
    half = MLA_ROPE_DIM // 2
    wh = w.reshape(r, MLA_HEADS, MLA_NOPE_DIM + MLA_ROPE_DIM)
    nope, pe = wh[..., :MLA_NOPE_DIM], wh[..., MLA_NOPE_DIM:]
    pad = jnp.zeros((r, MLA_HEADS, LANES - MLA_NOPE_DIM - MLA_ROPE_DIM), w.dtype)
    full = jnp.concatenate([nope, pe, pad], axis=-1)
    rot = jnp.concatenate([jnp.zeros_like(nope), pe[..., half:], pe[..., :half], pad], axis=-1)
    return (full.reshape(r, MLA_HEADS * LANES).astype(BF16),
            rot.reshape(r, MLA_HEADS * LANES).astype(BF16))


def _pack_kv_up(w):
    r = w.shape[0]
    wh = w.reshape(r, MLA_HEADS // 2, 2, MLA_NOPE_DIM + MLA_V_DIM)
    kn, v = wh[..., :MLA_NOPE_DIM], wh[..., MLA_NOPE_DIM:]
    zk = jnp.zeros_like(kn)
    k_full = jnp.concatenate([kn, zk], axis=-1)
    zv = jnp.zeros_like(v[:, :, 0])
    v_full = jnp.concatenate([v[:, :, 0], zv, zv, v[:, :, 1]], axis=-1)
    return (k_full.reshape(r, MLA_HEADS * LANES).astype(BF16),
            v_full.reshape(r, MLA_HEADS * LANES).astype(BF16))


def _rope_lane_patterns():
    half = MLA_ROPE_DIM // 2
    freqs = jnp.exp(-math.log(ROPE_THETA) * jnp.arange(half, dtype=F32) / half)
    zeros_lo = jnp.zeros((MLA_NOPE_DIM,), F32)
    zeros_hi = jnp.zeros((LANES - MLA_NOPE_DIM - MLA_ROPE_DIM,), F32)
    fpat = jnp.concatenate([zeros_lo, freqs, freqs, zeros_hi]).reshape(1, LANES)
    ones = jnp.ones((half,), F32)
    sgn = jnp.concatenate([zeros_lo, -ones, ones, zeros_hi]).reshape(1, LANES)
    return fpat, sgn


def kernel(x, positions, rel_bias, g_final, g_mix, w_in, swa_sinks, g_q_norm, w_q_up, g_kv_norm, w_kv_up, w_dw, b_dw, g_conv_ln, b_conv_ln, w_branch, w_out, g_mlp, w_up, w_down):
    b, s, d = x.shape
    t = b * s
    x2 = x.reshape(t, d)
    pos_col = positions.reshape(t, 1)
    fpat, sgn = _rope_lane_patterns()
    bias = _swa_bias(positions, rel_bias)
    for l in range(DEPTH):
        wq, wqr = _pack_q_up(w_q_up[l])
        wk, wv = _pack_kv_up(w_kv_up[l])
        qkv, u, gates, qf, kf, v = _in_proj(
            x2, g_mix[l].reshape(1, d), _pack_in_weights(w_in[l]), pos_col, fpat, sgn,
            g_q_norm[l].reshape(1, -1), g_kv_norm[l].reshape(1, -1), wq, wqr, wk, wv)
        y_a = _swa(qkv.reshape(b, s, _QKV_W), bias, swa_sinks[l])
        y_b = _conv(u.reshape(b, s, _U_W), w_dw[l].reshape(CONV_WIDTH, CONV_CHANNELS),
                    b_dw[l].reshape(1, -1), g_conv_ln[l].reshape(1, -1), b_conv_ln[l].reshape(1, -1))
        hw = MLA_HEADS * LANES
        y_c = _mla_attn(qf.reshape(b, s, hw), kf.reshape(b, s, hw),
                        v.reshape(b, s, hw))
        x2 = _merge(y_a.reshape(t, -1), y_b.reshape(t, -1), y_c.reshape(t, -1), gates, x2,
                    w_branch[l].astype(BF16), w_out[l].astype(BF16))
        x2 = _ffn(x2, g_mlp[l].reshape(1, d), w_up[l].astype(BF16), w_down[l].astype(BF16),
                  g_out=g_final.reshape(1, d) if l == DEPTH - 1 else None)
    return x2.reshape(b, s, d)
```

```python
import functools
import math

import jax
import jax.numpy as jnp
from jax import lax
from jax.experimental import pallas as pl
from jax.experimental.pallas import tpu as pltpu

D_MODEL = 1024
DEPTH = 4
BRANCH_WIDTH = 512
N_BRANCHES = 3
SWA_Q_HEADS = 8
SWA_KV_HEADS = 2
SWA_GROUP = SWA_Q_HEADS // SWA_KV_HEADS
SWA_HEAD_DIM = 64
WINDOW = 128
N_BUCKETS = 32
MAX_DISTANCE = 128
CONV_CHANNELS = 512
CONV_WIDTH = 31
MLA_HEADS = 8
MLA_Q_RANK = 256
MLA_KV_RANK = 128
MLA_NOPE_DIM = 64
MLA_ROPE_DIM = 32
MLA_V_DIM = 64
ROPE_THETA = 10000.0
D_FF = 4 * D_MODEL
EPS = 1e-6
NEG_INF = -1e30

LANES = 128
SUBLANES = 8
VMEM_LIMIT = 56 * 1024 * 1024

F32 = jnp.float32
BF16 = jnp.bfloat16

_MAX_EXACT = N_BUCKETS // 2
_T5_THRESHOLDS = tuple(range(1, _MAX_EXACT + 1)) + tuple(
    math.ceil(_MAX_EXACT * (MAX_DISTANCE / _MAX_EXACT) ** (j / (N_BUCKETS - _MAX_EXACT)))
    for j in range(1, N_BUCKETS - _MAX_EXACT))

_QKV_W = 768
_U_W = 2 * CONV_CHANNELS
_CMLA_W = MLA_Q_RANK + MLA_KV_RANK + 2 * LANES
_GATE_W = N_BRANCHES * D_MODEL
_IN_W = _QKV_W + _U_W + _CMLA_W + _GATE_W


def _params(*semantics):
    return pltpu.CompilerParams(dimension_semantics=semantics, vmem_limit_bytes=VMEM_LIMIT)


def _rms(xf, g):
    return xf * lax.rsqrt(jnp.mean(xf * xf, axis=-1, keepdims=True) + EPS) * g


def _bias_kernel(posq_ref, pkp_ref, pkc_ref, tbl_ref, out_ref):
    blk = pl.program_id(1)
    pk_prev = jnp.where(blk > 0, pkp_ref[0], 0)
    pk = jnp.concatenate([pk_prev, pkc_ref[0]], axis=1)
    ki = lax.broadcasted_iota(jnp.int32, (SUBLANES, 2 * WINDOW), 1)
    has_prev = (blk > 0) | (ki >= WINDOW)

    dist = lax.broadcasted_iota(jnp.int32, (SUBLANES, LANES), 1)
    by_dist = [tbl_ref[pl.ds(h, 1), 0:LANES] + jnp.zeros((SUBLANES, LANES), F32) for h in range(SWA_Q_HEADS)]
    for b in range(1, N_BUCKETS):
        ge = dist >= _T5_THRESHOLDS[b - 1]
        for h in range(SWA_Q_HEADS):
            by_dist[h] = jnp.where(ge, tbl_ref[pl.ds(b * SWA_Q_HEADS + h, 1), 0:LANES], by_dist[h])

    def rows(r, carry):
        r0 = pl.multiple_of(r * SUBLANES, SUBLANES)
        pq = posq_ref[0, pl.ds(r0, SUBLANES), :]
        n = jnp.minimum(jnp.maximum(pq - pk, 0), LANES - 1)
        qi = WINDOW + r0 + lax.broadcasted_iota(jnp.int32, (SUBLANES, 2 * WINDOW), 0)
        mask = (ki <= qi) & (qi - ki < WINDOW) & has_prev
        for h in range(SWA_Q_HEADS):
            val = jnp.concatenate(
                [jnp.take_along_axis(by_dist[h], n[:, half * LANES:(half + 1) * LANES], axis=1)
                 for half in range(2 * WINDOW // LANES)], axis=1)
            out_ref[0, 0, h, pl.ds(r0, SUBLANES), :] = jnp.where(mask, val, NEG_INF)
        return carry

    lax.fori_loop(0, WINDOW // SUBLANES, rows, 0, unroll=8)


def _swa_bias(positions, rel_bias):
    b, s = positions.shape
    nb = s // WINDOW
    tbl = jnp.broadcast_to(rel_bias.astype(F32).reshape(N_BUCKETS * SWA_Q_HEADS, 1),
                           (N_BUCKETS * SWA_Q_HEADS, 2 * WINDOW))
    pos_col = positions.reshape(b, s, 1)
    pos_row = positions.reshape(b, 1, s)
    return pl.pallas_call(
        _bias_kernel,
        grid=(b, nb),
        in_specs=[
            pl.BlockSpec((1, WINDOW, 1), lambda i, n: (i, n, 0)),
            pl.BlockSpec((1, 1, WINDOW), lambda i, n: (i, 0, jnp.maximum(n - 1, 0))),
            pl.BlockSpec((1, 1, WINDOW), lambda i, n: (i, 0, n)),
            pl.BlockSpec((N_BUCKETS * SWA_Q_HEADS, 2 * WINDOW), lambda i, n: (0, 0)),
        ],
        out_specs=pl.BlockSpec((1, 1, SWA_Q_HEADS, WINDOW, 2 * WINDOW), lambda i, n: (i, n, 0, 0, 0)),
        out_shape=jax.ShapeDtypeStruct((b, nb, SWA_Q_HEADS, WINDOW, 2 * WINDOW), F32),
        compiler_params=_params("parallel", "parallel"),
        name="swa_bias",
    )(pos_col, pos_row, pos_row, tbl)


def _in_proj_kernel(x_ref, g_ref, w_ref, pos_ref, fpat_ref, sgn_ref, gq_ref, gkv_ref,
                    wq_ref, wqr_ref, wk_ref, wv_ref,
                    qkv_ref, u_ref, gate_ref, q_ref, k_ref, v_ref, *, chunk):
    h = _rms(x_ref[...], g_ref[...]).astype(BF16)

    def project(col, width, out_ref):
        for c0 in range(0, width, chunk):
            c1 = min(c0 + chunk, width)
            out_ref[:, c0:c1] = jnp.dot(h, w_ref[:, col + c0:col + c1],
                                        preferred_element_type=F32).astype(out_ref.dtype)

    project(0, _QKV_W, qkv_ref)
    project(_QKV_W, _U_W, u_ref)
    c_col = _QKV_W + _U_W
    project(c_col + _CMLA_W, _GATE_W, gate_ref)
    c = jnp.dot(h, w_ref[:, c_col:c_col + _CMLA_W], preferred_element_type=F32)
    _mla_prep(c, pos_ref, fpat_ref, sgn_ref, gq_ref, gkv_ref, wq_ref, wqr_ref, wk_ref, wv_ref,
              q_ref, k_ref, v_ref)


def _in_proj(x2, g, w, pos_col, fpat, sgn, gq, gkv, wq, wqr, wk, wv, tm=512):
    t = x2.shape[0]
    row = lambda i: (i, 0)
    const = lambda i: (0, 0)
    hw = MLA_HEADS * LANES
    return pl.pallas_call(
        functools.partial(_in_proj_kernel, chunk=512),
        grid=(t // tm,),
        in_specs=[
            pl.BlockSpec((tm, D_MODEL), row),
            pl.BlockSpec((1, D_MODEL), const),
            pl.BlockSpec((D_MODEL, _IN_W), const),
            pl.BlockSpec((tm, 1), row),
            pl.BlockSpec((1, LANES), const),
            pl.BlockSpec((1, LANES), const),
            pl.BlockSpec((1, MLA_Q_RANK), const),
            pl.BlockSpec((1, MLA_KV_RANK), const),
            pl.BlockSpec((MLA_Q_RANK, hw), const),
            pl.BlockSpec((MLA_Q_RANK, hw), const),
            pl.BlockSpec((MLA_KV_RANK, hw), const),
            pl.BlockSpec((MLA_KV_RANK, hw), const),
        ],
        out_specs=[
            pl.BlockSpec((tm, _QKV_W), row),
            pl.BlockSpec((tm, _U_W), row),
            pl.BlockSpec((tm, _GATE_W), row),
            pl.BlockSpec((tm, hw), row),
            pl.BlockSpec((tm, hw), row),
            pl.BlockSpec((tm, hw), row),
        ],
        out_shape=[
            jax.ShapeDtypeStruct((t, _QKV_W), BF16),
            jax.ShapeDtypeStruct((t, _U_W), F32),
            jax.ShapeDtypeStruct((t, _GATE_W), BF16),
            jax.ShapeDtypeStruct((t, hw), BF16),
            jax.ShapeDtypeStruct((t, hw), BF16),
            jax.ShapeDtypeStruct((t, hw), BF16),
        ],
        compiler_params=_params("parallel"),
        name="in_proj",
    )(x2, g, w, pos_col, fpat, sgn, gq, gkv, wq, wqr, wk, wv)


def _swa_kernel(q_ref, kp_ref, kc_ref, vp_ref, vc_ref, bias_ref, sink_ref, o_ref, *, nw):
    k_all = jnp.concatenate([kp_ref[0], kc_ref[0]], axis=0)
    v_all = jnp.concatenate([vp_ref[0], vc_ref[0]], axis=0)
    scale = SWA_HEAD_DIM ** -0.5
    for w in range(nw):
        rows = slice(w * WINDOW, (w + 1) * WINDOW)
        k = k_all[w * WINDOW:(w + 2) * WINDOW]
        v = v_all[w * WINDOW:(w + 2) * WINDOW]
        outs = []
        for h in range(SWA_Q_HEADS):
            hk = h // SWA_GROUP
            qh = q_ref[0, rows, h * SWA_HEAD_DIM:(h + 1) * SWA_HEAD_DIM]
            kh = k[:, hk * SWA_HEAD_DIM:(hk + 1) * SWA_HEAD_DIM]
            vh = v[:, hk * SWA_HEAD_DIM:(hk + 1) * SWA_HEAD_DIM]
            s = lax.dot_general(qh, kh, (((1,), (1,)), ((), ())), preferred_element_type=F32)
            s = s * scale + bias_ref[0, w, h]
            sink = sink_ref[h]
            m = jnp.maximum(jnp.max(s, axis=1, keepdims=True), sink)
            p = jnp.exp(s - m)
            denom = jnp.sum(p, axis=1, keepdims=True) + jnp.exp(sink - m)
            o = jnp.dot(p.astype(BF16), vh, preferred_element_type=F32)
            outs.append(o / denom)
        o_ref[0, rows, :] = jnp.concatenate(outs, axis=1).astype(o_ref.dtype)


def _swa(qkv, bias, sinks, nw=2):
    b, s, _ = qkv.shape
    nb = s // WINDOW
    kcol = (SWA_Q_HEADS * SWA_HEAD_DIM) // LANES
    vcol = kcol + 1
    prev = lambda n: jnp.maximum(n * nw - 1, 0)
    tq = nw * WINDOW
    return pl.pallas_call(
        functools.partial(_swa_kernel, nw=nw),
        grid=(b, nb // nw),
        in_specs=[
            pl.BlockSpec((1, tq, SWA_Q_HEADS * SWA_HEAD_DIM), lambda i, n: (i, n, 0)),
            pl.BlockSpec((1, WINDOW, LANES), lambda i, n: (i, prev(n), kcol)),
            pl.BlockSpec((1, tq, LANES), lambda i, n: (i, n, kcol)),
            pl.BlockSpec((1, WINDOW, LANES), lambda i, n: (i, prev(n), vcol)),
            pl.BlockSpec((1, tq, LANES), lambda i, n: (i, n, vcol)),
            pl.BlockSpec((1, nw, SWA_Q_HEADS, WINDOW, 2 * WINDOW), lambda i, n: (i, n, 0, 0, 0)),
            pl.BlockSpec(memory_space=pltpu.SMEM),
        ],
        out_specs=pl.BlockSpec((1, tq, BRANCH_WIDTH), lambda i, n: (i, n, 0)),
        out_shape=jax.ShapeDtypeStruct((b, s, BRANCH_WIDTH), BF16),
        compiler_params=_params("parallel", "parallel"),
        name="swa_attn",
    )(qkv, qkv, qkv, qkv, qkv, bias, sinks)


_CONV_HALO = 32
_CONV_CHUNK = 32


def _conv_kernel(uh_ref, uc_ref, w_ref, bdw_ref, g_ref, b_ref, o_ref, ext_ref, sh_ref, *, ts):
    c = CONV_CHANNELS

    def glu(u):
        return u[:, :c] * jax.nn.sigmoid(u[:, c:])

    halo = glu(uh_ref[0])
    ext_ref[0:_CONV_HALO, :] = jnp.where(pl.program_id(1) > 0, halo, 0.0)
    ext_ref[_CONV_HALO:, :] = glu(uc_ref[0])
    span = sh_ref.shape[1]
    for r in range(1, SUBLANES):
        sh_ref[r - 1] = ext_ref[r:r + span, :]
    lead = _CONV_HALO - (CONV_WIDTH - 1)
    for r0 in range(0, ts, _CONV_CHUNK):
        acc = jnp.broadcast_to(bdw_ref[...], (_CONV_CHUNK, c))
        for j in range(CONV_WIDTH):
            aligned, r = divmod(lead + j, SUBLANES)
            start = r0 + aligned * SUBLANES
            if r == 0:
                tap = ext_ref[start:start + _CONV_CHUNK, :]
            else:
                tap = sh_ref[r - 1, start:start + _CONV_CHUNK, :]
            acc = acc + w_ref[j:j + 1, :] * tap
        mu = jnp.mean(acc, axis=-1, keepdims=True)
        d = acc - mu
        var = jnp.mean(d * d, axis=-1, keepdims=True)
        y = d * lax.rsqrt(var + EPS) * g_ref[...] + b_ref[...]
        o_ref[0, r0:r0 + _CONV_CHUNK, :] = (y * jax.nn.sigmoid(y)).astype(o_ref.dtype)


def _conv(u, w_dw, b_dw, g_ln, b_ln, ts=512):
    b, s, _ = u.shape
    ratio = ts // _CONV_HALO
    const = lambda i, n: (0, 0)
    return pl.pallas_call(
        functools.partial(_conv_kernel, ts=ts),
        grid=(b, s // ts),
        in_specs=[
            pl.BlockSpec((1, _CONV_HALO, _U_W), lambda i, n: (i, jnp.maximum(n * ratio - 1, 0), 0)),
            pl.BlockSpec((1, ts, _U_W), lambda i, n: (i, n, 0)),
            pl.BlockSpec((CONV_WIDTH, CONV_CHANNELS), const),
            pl.BlockSpec((1, CONV_CHANNELS), const),
            pl.BlockSpec((1, CONV_CHANNELS), const),
            pl.BlockSpec((1, CONV_CHANNELS), const),
        ],
        out_specs=pl.BlockSpec((1, ts, CONV_CHANNELS), lambda i, n: (i, n, 0)),
        out_shape=jax.ShapeDtypeStruct((b, s, CONV_CHANNELS), BF16),
        scratch_shapes=[pltpu.VMEM((ts + _CONV_HALO, CONV_CHANNELS), F32),
                        pltpu.VMEM((SUBLANES - 1, ts + _CONV_HALO - SUBLANES, CONV_CHANNELS), F32)],
        compiler_params=_params("parallel", "parallel"),
        name="conformer_conv",
    )(u, u, w_dw, b_dw, g_ln, b_ln)


def _mla_prep(c, pos_ref, fpat_ref, sgn_ref, gq_ref, gkv_ref,
              wq_ref, wqr_ref, wk_ref, wv_ref, q_ref, k_ref, v_ref):
    cq = c[:, 0:MLA_Q_RANK]
    ckv = c[:, MLA_Q_RANK:MLA_Q_RANK + MLA_KV_RANK]
    kpe_a = c[:, MLA_Q_RANK + MLA_KV_RANK:MLA_Q_RANK + MLA_KV_RANK + LANES]
    kpe_b = c[:, MLA_Q_RANK + MLA_KV_RANK + LANES:]
    ang = pos_ref[...].astype(F32) * fpat_ref[...]
    sgn = sgn_ref[...]
    cos = jnp.where(sgn != 0.0, jnp.cos(ang), 1.0)
    sin = jnp.sin(ang) * sgn
    hq = _rms(cq, gq_ref[...]).astype(BF16)
    hkv = _rms(ckv, gkv_ref[...]).astype(BF16)
    k_rope = kpe_a * cos + kpe_b * sin
    q_scale = (MLA_NOPE_DIM + MLA_ROPE_DIM) ** -0.5 * math.log2(math.e)
    cos_q = cos * q_scale
    sin_q = sin * q_scale
    qf = jnp.dot(hq, wq_ref[...], preferred_element_type=F32)
    qr = jnp.dot(hq, wqr_ref[...], preferred_element_type=F32)
    kn = jnp.dot(hkv, wk_ref[...], preferred_element_type=F32)
    for h in range(MLA_HEADS):
        sl = slice(h * LANES, (h + 1) * LANES)
        q_ref[:, sl] = (qf[:, sl] * cos_q + qr[:, sl] * sin_q).astype(q_ref.dtype)
        k_ref[:, sl] = (kn[:, sl] + k_rope).astype(k_ref.dtype)
    half_block = lax.broadcasted_iota(jnp.int32, (1, v_ref.shape[1]), 1) // MLA_V_DIM
    ones = ((half_block % 4 == 1) | (half_block % 4 == 2)).astype(F32)
    v_ref[...] = (jnp.dot(hkv, wv_ref[...], preferred_element_type=F32) + ones).astype(v_ref.dtype)


def _mla_attn_kernel(q_ref, k_ref, v_ref, o_ref, sa_ref, sb_ref, m_ref, acc_ref, *, t, nh, nq):
    base = pl.program_id(2) * nq

    def init():
        m_ref[...] = jnp.full(m_ref.shape, NEG_INF, F32)
        acc_ref[...] = jnp.zeros(acc_ref.shape, F32)

    def scores(j, s_ref, qi):
        k0 = pl.multiple_of(j * t, t)
        for hh in range(nh):
            lanes = slice(hh * LANES, (hh + 1) * LANES)
            s_ref[hh] = lax.dot_general(q_ref[0, qi * t:(qi + 1) * t, lanes], k_ref[0, pl.ds(k0, t), lanes],
                                        (((1,), (1,)), ((), ())), preferred_element_type=F32)

    def consume(j, s_ref, diagonal):
        k0 = pl.multiple_of(j * t, t)
        for hh in range(nh):
            lanes = slice(hh * LANES, (hh + 1) * LANES)
            s = s_ref[hh]
            if diagonal:
                row = lax.broadcasted_iota(jnp.int32, (t, t), 0)
                col = lax.broadcasted_iota(jnp.int32, (t, t), 1)
                s = jnp.where(col <= row, s, NEG_INF)
            m_old = m_ref[hh]
            m_new = jnp.maximum(m_old, jnp.max(s, axis=1, keepdims=True))
            alpha = jnp.exp2(m_old - m_new)
            p = jnp.exp2(s - jnp.concatenate([m_new] * (t // LANES), axis=1))
            m_ref[hh] = m_new
            acc_ref[hh] = alpha * acc_ref[hh] + jnp.dot(
                p.astype(BF16), v_ref[0, pl.ds(k0, t), lanes], preferred_element_type=F32)

    def finish(qi):
        lane = lax.broadcasted_iota(jnp.int32, (t, LANES), 1)
        for pair in range(nh // 2):
            even, odd = acc_ref[2 * pair], acc_ref[2 * pair + 1]
            acc = jnp.where(lane < MLA_V_DIM, even, odd)
            den = jnp.where(lane < MLA_V_DIM, pltpu.roll(even, MLA_V_DIM, 1), pltpu.roll(odd, MLA_V_DIM, 1))
            o_ref[0, qi * t:(qi + 1) * t, pair * LANES:(pair + 1) * LANES] = (acc / den).astype(o_ref.dtype)

    def sweep(first, other, qi, n_pairs):
        def pair(jj, carry):
            j = 2 * jj
            scores(j + 1, other, qi)
            consume(j, first, diagonal=False)
            scores(j + 2, first, qi)
            consume(j + 1, other, diagonal=False)
            return carry

        lax.fori_loop(0, n_pairs, pair, 0)

    first, other = sa_ref, sb_ref
    scores(0, first, 0)
    for qi in range(nq):
        n = base + qi
        init()
        sweep(first, other, qi, n // 2)
        if qi % 2 == 0:
            if qi + 1 < nq:
                scores(0, other, qi + 1)
            consume(n, first, diagonal=True)
            first, other = other, first
        else:
            scores(n, other, qi)
            consume(n - 1, first, diagonal=False)
            if qi + 1 < nq:
                scores(0, first, qi + 1)
            consume(n, other, diagonal=True)
        finish(qi)


def _mla_attn(qf, kf, v, t=512, nh=4, nq=4):
    b, s, _ = qf.shape
    width = nh * LANES
    return pl.pallas_call(
        functools.partial(_mla_attn_kernel, t=t, nh=nh, nq=nq),
        grid=(b, MLA_HEADS // nh, s // (nq * t)),
        in_specs=[
            pl.BlockSpec((1, nq * t, width), lambda i, p, n: (i, n, p)),
            pl.BlockSpec((1, s, width), lambda i, p, n: (i, 0, p)),
            pl.BlockSpec((1, s, width), lambda i, p, n: (i, 0, p)),
        ],
        out_specs=pl.BlockSpec((1, nq * t, nh * MLA_V_DIM), lambda i, p, n: (i, n, p)),
        out_shape=jax.ShapeDtypeStruct((b, s, MLA_HEADS * MLA_V_DIM), BF16),
        scratch_shapes=[pltpu.VMEM((nh, t, t), F32), pltpu.VMEM((nh, t, t), F32),
                        pltpu.VMEM((nh, t, LANES), F32), pltpu.VMEM((nh, t, LANES), F32)],
        compiler_params=_params("parallel", "parallel", "arbitrary"),
        name="mla_attn",
    )(qf, kf, v)


def _merge_kernel(ya_ref, yb_ref, yc_ref, gate_ref, x_ref, wb_ref, wo_ref, o_ref):
    merged = None
    for n, y_ref in enumerate((ya_ref, yb_ref, yc_ref)):
        br = jnp.dot(y_ref[...], wb_ref[n], preferred_element_type=F32)
        term = jax.nn.sigmoid(gate_ref[:, n * D_MODEL:(n + 1) * D_MODEL].astype(F32)) * br
        merged = term if merged is None else merged + term
    o_ref[...] = x_ref[...] + jnp.dot(merged.astype(BF16), wo_ref[...], preferred_element_type=F32)


def _merge(ya, yb, yc, gates, x2, wb, wo, tm=512):
    t = x2.shape[0]
    row = lambda i: (i, 0)
    return pl.pallas_call(
        _merge_kernel,
        grid=(t // tm,),
        in_specs=[
            pl.BlockSpec((tm, BRANCH_WIDTH), row),
            pl.BlockSpec((tm, BRANCH_WIDTH), row),
            pl.BlockSpec((tm, BRANCH_WIDTH), row),
            pl.BlockSpec((tm, _GATE_W), row),
            pl.BlockSpec((tm, D_MODEL), row),
            pl.BlockSpec((N_BRANCHES, BRANCH_WIDTH, D_MODEL), lambda i: (0, 0, 0)),
            pl.BlockSpec((D_MODEL, D_MODEL), lambda i: (0, 0)),
        ],
        out_specs=pl.BlockSpec((tm, D_MODEL), row),
        out_shape=jax.ShapeDtypeStruct((t, D_MODEL), F32),
        compiler_params=_params("parallel"),
        name="merge_out",
    )(ya, yb, yc, gates, x2, wb, wo)


def _ffn_kernel(x_ref, g_ref, wu_ref, wd_ref, *rest, out_norm):
    if out_norm:
        gout_ref, o_ref, h_ref = rest
    else:
        o_ref, h_ref = rest
    f = pl.program_id(1)

    @pl.when(f == 0)
    def _():
        xf = x_ref[...]
        h_ref[...] = _rms(xf, g_ref[...]).astype(BF16)
        o_ref[...] = xf

    up = jnp.dot(h_ref[...], wu_ref[...], preferred_element_type=F32)
    act = jnp.square(jnp.maximum(up, 0.0)).astype(BF16)
    o_ref[...] += jnp.dot(act, wd_ref[...], preferred_element_type=F32)

    if out_norm:
        @pl.when(f == pl.num_programs(1) - 1)
        def _():
            o_ref[...] = _rms(o_ref[...], gout_ref[...])


def _ffn(x2, g, wu, wd, g_out=None, tm=1024, tf=1024):
    t = x2.shape[0]
    vec = pl.BlockSpec((1, D_MODEL), lambda i, f: (0, 0))
    in_specs = [
        pl.BlockSpec((tm, D_MODEL), lambda i, f: (i, 0)),
        vec,
        pl.BlockSpec((D_MODEL, tf), lambda i, f: (0, f)),
        pl.BlockSpec((tf, D_MODEL), lambda i, f: (f, 0)),
    ]
    args = [x2, g, wu, wd]
    if g_out is not None:
        in_specs.append(vec)
        args.append(g_out)
    return pl.pallas_call(
        functools.partial(_ffn_kernel, out_norm=g_out is not None),
        grid=(t // tm, D_FF // tf),
        in_specs=in_specs,
        out_specs=pl.BlockSpec((tm, D_MODEL), lambda i, f: (i, 0)),
        out_shape=jax.ShapeDtypeStruct((t, D_MODEL), F32),
        scratch_shapes=[pltpu.VMEM((tm, D_MODEL), BF16)],
        compiler_params=_params("parallel", "arbitrary"),
        name="ffn",
    )(*args)


def _pack_in_weights(w):
    q_end = _QKV_W
    u_end = q_end + _U_W
    cq_end = u_end + MLA_Q_RANK
    ckv_end = cq_end + MLA_KV_RANK
    kpe_end = ckv_end + MLA_ROPE_DIM
    half = MLA_ROPE_DIM // 2
    kpe = w[:, ckv_end:kpe_end]
    z_lo = jnp.zeros((w.shape[0], MLA_NOPE_DIM), w.dtype)
    z_hi = jnp.zeros((w.shape[0], LANES - MLA_NOPE_DIM - MLA_ROPE_DIM), w.dtype)
    kpe_a = jnp.concatenate([z_lo, kpe, z_hi], axis=1)
    kpe_b = jnp.concatenate([z_lo, kpe[:, half:], kpe[:, :half], z_hi], axis=1)
    return jnp.concatenate([w[:, :ckv_end], kpe_a, kpe_b, w[:, kpe_end:]], axis=1).astype(BF16)


def _pack_q_up(w):
    r = w.shape[0]
    half = MLA_ROPE_DIM // 2
    wh = w.reshape(r, MLA_HEADS, MLA_NOPE_DIM + MLA_ROPE_DIM)
    nope, pe = wh[..., :MLA_NOPE_DIM], wh[..., MLA_NOPE_DIM:]
    pad = jnp.zeros((r, MLA_HEADS, LANES - MLA_NOPE_DIM - MLA_ROPE_DIM), w.dtype)
    full = jnp.concatenate([nope, pe, pad], axis=-1)
    rot = jnp.concatenate([jnp.zeros_like(nope), pe[..., half:], pe[..., :half], pad], axis=-1)
    return (full.reshape(r, MLA_HEADS * LANES).astype(BF16),
            rot.reshape(r, MLA_HEADS * LANES).astype(BF16))


def _pack_kv_up(w):
    r = w.shape[0]
    wh = w.reshape(r, MLA_HEADS // 2, 2, MLA_NOPE_DIM + MLA_V_DIM)
    kn, v = wh[..., :MLA_NOPE_DIM], wh[..., MLA_NOPE_DIM:]
    zk = jnp.zeros_like(kn)
    k_full = jnp.concatenate([kn, zk], axis=-1)
    zv = jnp.zeros_like(v[:, :, 0])
    v_full = jnp.concatenate([v[:, :, 0], zv, zv, v[:, :, 1]], axis=-1)
    return (k_full.reshape(r, MLA_HEADS * LANES).astype(BF16),
            v_full.reshape(r, MLA_HEADS * LANES).astype(BF16))


def _rope_lane_patterns():
    half = MLA_ROPE_DIM // 2
    freqs = jnp.exp(-math.log(ROPE_THETA) * jnp.arange(half, dtype=F32) / half)
    zeros_lo = jnp.zeros((MLA_NOPE_DIM,), F32)
    zeros_hi = jnp.zeros((LANES - MLA_NOPE_DIM - MLA_ROPE_DIM,), F32)
    fpat = jnp.concatenate([zeros_lo, freqs, freqs, zeros_hi]).reshape(1, LANES)
    ones = jnp.ones((half,), F32)
    sgn = jnp.concatenate([zeros_lo, -ones, ones, zeros_hi]).reshape(1, LANES)
    return fpat, sgn


def kernel(x, positions, rel_bias, g_final, g_mix, w_in, swa_sinks, g_q_norm, w_q_up, g_kv_norm, w_kv_up, w_dw, b_dw, g_conv_ln, b_conv_ln, w_branch, w_out, g_mlp, w_up, w_down):
    b, s, d = x.shape
    t = b * s
    x2 = x.reshape(t, d)
    pos_col = positions.reshape(t, 1)
    fpat, sgn = _rope_lane_patterns()
    bias = _swa_bias(positions, rel_bias)
    for l in range(DEPTH):
        wq, wqr = _pack_q_up(w_q_up[l])
        wk, wv = _pack_kv_up(w_kv_up[l])
        qkv, u, gates, qf, kf, v = _in_proj(
            x2, g_mix[l].reshape(1, d), _pack_in_weights(w_in[l]), pos_col, fpat, sgn,
            g_q_norm[l].reshape(1, -1), g_kv_norm[l].reshape(1, -1), wq, wqr, wk, wv)
        y_a = _swa(qkv.reshape(b, s, _QKV_W), bias, swa_sinks[l])
        y_b = _conv(u.reshape(b, s, _U_W), w_dw[l].reshape(CONV_WIDTH, CONV_CHANNELS),
                    b_dw[l].reshape(1, -1), g_conv_ln[l].reshape(1, -1), b_conv_ln[l].reshape(1, -1))
        hw = MLA_HEADS * LANES
        y_c = _mla_attn(qf.reshape(b, s, hw), kf.reshape(b, s, hw),
                        v.reshape(b, s, hw))
        x2 = _merge(y_a.reshape(t, -1), y_b.reshape(t, -1), y_c.reshape(t, -1), gates, x2,
                    w_branch[l].astype(BF16), w_out[l].astype(BF16))
        x2 = _ffn(x2, g_mlp[l].reshape(1, d), w_up[l].astype(BF16), w_down[l].astype(BF16),
                  g_out=g_final.reshape(1, d) if l == DEPTH - 1 else None)
    return x2.reshape(b, s, d)
```

```python
import functools
import math

import jax
import jax.numpy as jnp
from jax import lax
from jax.experimental import pallas as pl
from jax.experimental.pallas import tpu as pltpu

D_MODEL = 1024
DEPTH = 4
BRANCH_WIDTH = 512
N_BRANCHES = 3
SWA_Q_HEADS = 8
SWA_KV_HEADS = 2
SWA_GROUP = SWA_Q_HEADS // SWA_KV_HEADS
SWA_HEAD_DIM = 64
WINDOW = 128
N_BUCKETS = 32
MAX_DISTANCE = 128
CONV_CHANNELS = 512
CONV_WIDTH = 31
MLA_HEADS = 8
MLA_Q_RANK = 256
MLA_KV_RANK = 128
MLA_NOPE_DIM = 64
MLA_ROPE_DIM = 32
MLA_V_DIM = 64
ROPE_THETA = 10000.0
D_FF = 4 * D_MODEL
EPS = 1e-6
NEG_INF = -1e30

LANES = 128
SUBLANES = 8
VMEM_LIMIT = 56 * 1024 * 1024

F32 = jnp.float32
BF16 = jnp.bfloat16

_MAX_EXACT = N_BUCKETS // 2
_T5_THRESHOLDS = tuple(range(1, _MAX_EXACT + 1)) + tuple(
    math.ceil(_MAX_EXACT * (MAX_DISTANCE / _MAX_EXACT) ** (j / (N_BUCKETS - _MAX_EXACT)))
    for j in range(1, N_BUCKETS - _MAX_EXACT))

_QKV_W = 768
_U_W = 2 * CONV_CHANNELS
_CMLA_W = MLA_Q_RANK + MLA_KV_RANK + 2 * LANES
_GATE_W = N_BRANCHES * D_MODEL
_IN_W = _QKV_W + _U_W + _CMLA_W + _GATE_W


def _params(*semantics):
    return pltpu.CompilerParams(dimension_semantics=semantics, vmem_limit_bytes=VMEM_LIMIT)


def _rms(xf, g):
    return xf * lax.rsqrt(jnp.mean(xf * xf, axis=-1, keepdims=True) + EPS) * g


def _bias_kernel(posq_ref, pkp_ref, pkc_ref, tbl_ref, out_ref, *, nw):
    first_step = pl.program_id(1) == 0
    pk_all = jnp.concatenate([jnp.where(first_step, 0, pkp_ref[0]), pkc_ref[0]], axis=1)
    ki = lax.broadcasted_iota(jnp.int32, (SUBLANES, 2 * WINDOW), 1)

    dist = lax.broadcasted_iota(jnp.int32, (SUBLANES, LANES), 1)
    by_dist = [tbl_ref[pl.ds(h, 1), 0:LANES] + jnp.zeros((SUBLANES, LANES), F32) for h in range(SWA_Q_HEADS)]
    for b in range(1, N_BUCKETS):
        ge = dist >= _T5_THRESHOLDS[b - 1]
        for h in range(SWA_Q_HEADS):
            by_dist[h] = jnp.where(ge, tbl_ref[pl.ds(b * SWA_Q_HEADS + h, 1), 0:LANES], by_dist[h])

    for w in range(nw):
        pk = pk_all[:, w * WINDOW:(w + 2) * WINDOW]
        has_prev = (ki >= 0) if w else (jnp.logical_not(first_step) | (ki >= WINDOW))

        def rows(r, carry, w=w, pk=pk, has_prev=has_prev):
            r0 = pl.multiple_of(r * SUBLANES, SUBLANES)
            pq = posq_ref[0, pl.ds(w * WINDOW + r0, SUBLANES), :]
            n = jnp.minimum(jnp.maximum(pq - pk, 0), LANES - 1)
            qi = WINDOW + r0 + lax.broadcasted_iota(jnp.int32, (SUBLANES, 2 * WINDOW), 0)
            mask = (ki <= qi) & (qi - ki < WINDOW) & has_prev
            for h in range(SWA_Q_HEADS):
                val = jnp.concatenate(
                    [jnp.take_along_axis(by_dist[h], n[:, half * LANES:(half + 1) * LANES], axis=1)
                     for half in range(2 * WINDOW // LANES)], axis=1)
                out_ref[0, w, h, pl.ds(r0, SUBLANES), :] = jnp.where(mask, val, NEG_INF)
            return carry

        lax.fori_loop(0, WINDOW // SUBLANES, rows, 0, unroll=8)


def _swa_bias(positions, rel_bias, nw=4):
    b, s = positions.shape
    nb = s // WINDOW
    tbl = jnp.broadcast_to(rel_bias.astype(F32).reshape(N_BUCKETS * SWA_Q_HEADS, 1),
                           (N_BUCKETS * SWA_Q_HEADS, 2 * WINDOW))
    pos_col = positions.reshape(b, s, 1)
    pos_row = positions.reshape(b, 1, s)
    return pl.pallas_call(
        functools.partial(_bias_kernel, nw=nw),
        grid=(b, nb // nw),
        in_specs=[
            pl.BlockSpec((1, nw * WINDOW, 1), lambda i, n: (i, n, 0)),
            pl.BlockSpec((1, 1, WINDOW), lambda i, n: (i, 0, jnp.maximum(n * nw - 1, 0))),
            pl.BlockSpec((1, 1, nw * WINDOW), lambda i, n: (i, 0, n)),
            pl.BlockSpec((N_BUCKETS * SWA_Q_HEADS, 2 * WINDOW), lambda i, n: (0, 0)),
        ],
        out_specs=pl.BlockSpec((1, nw, SWA_Q_HEADS, WINDOW, 2 * WINDOW), lambda i, n: (i, n, 0, 0, 0)),
        out_shape=jax.ShapeDtypeStruct((b, nb, SWA_Q_HEADS, WINDOW, 2 * WINDOW), F32),
        compiler_params=_params("parallel", "parallel"),
        name="swa_bias",
    )(pos_col, pos_row, pos_row, tbl)


def _in_proj_kernel(x_ref, g_ref, w_ref, pos_ref, fpat_ref, sgn_ref, gq_ref, gkv_ref,
                    wq_ref, wqr_ref, wk_ref, wv_ref,
                    qkv_ref, u_ref, gate_ref, q_ref, k_ref, v_ref, *, chunk):
    h = _rms(x_ref[...], g_ref[...]).astype(BF16)

    def project(col, width, out_ref):
        for c0 in range(0, width, chunk):
            c1 = min(c0 + chunk, width)
            out_ref[:, c0:c1] = jnp.dot(h, w_ref[:, col + c0:col + c1],
                                        preferred_element_type=F32).astype(out_ref.dtype)

    project(0, _QKV_W, qkv_ref)
    project(_QKV_W, _U_W, u_ref)
    c_col = _QKV_W + _U_W
    project(c_col + _CMLA_W, _GATE_W, gate_ref)
    c = jnp.dot(h, w_ref[:, c_col:c_col + _CMLA_W], preferred_element_type=F32)
    _mla_prep(c, pos_ref, fpat_ref, sgn_ref, gq_ref, gkv_ref, wq_ref, wqr_ref, wk_ref, wv_ref,
              q_ref, k_ref, v_ref)


def _in_proj(x2, g, w, pos_col, fpat, sgn, gq, gkv, wq, wqr, wk, wv, tm=512):
    t = x2.shape[0]
    row = lambda i: (i, 0)
    const = lambda i: (0, 0)
    hw = MLA_HEADS * LANES
    return pl.pallas_call(
        functools.partial(_in_proj_kernel, chunk=512),
        grid=(t // tm,),
        in_specs=[
            pl.BlockSpec((tm, D_MODEL), row),
            pl.BlockSpec((1, D_MODEL), const),
            pl.BlockSpec((D_MODEL, _IN_W), const),
            pl.BlockSpec((tm, 1), row),
            pl.BlockSpec((1, LANES), const),
            pl.BlockSpec((1, LANES), const),
            pl.BlockSpec((1, MLA_Q_RANK), const),
            pl.BlockSpec((1, MLA_KV_RANK), const),
            pl.BlockSpec((MLA_Q_RANK, hw), const),
            pl.BlockSpec((MLA_Q_RANK, hw), const),
            pl.BlockSpec((MLA_KV_RANK, hw), const),
            pl.BlockSpec((MLA_KV_RANK, hw), const),
        ],
        out_specs=[
            pl.BlockSpec((tm, _QKV_W), row),
            pl.BlockSpec((tm, _U_W), row),
            pl.BlockSpec((tm, _GATE_W), row),
            pl.BlockSpec((tm, hw), row),
            pl.BlockSpec((tm, hw), row),
            pl.BlockSpec((tm, hw), row),
        ],
        out_shape=[
            jax.ShapeDtypeStruct((t, _QKV_W), BF16),
            jax.ShapeDtypeStruct((t, _U_W), F32),
            jax.ShapeDtypeStruct((t, _GATE_W), BF16),
            jax.ShapeDtypeStruct((t, hw), BF16),
            jax.ShapeDtypeStruct((t, hw), BF16),
            jax.ShapeDtypeStruct((t, hw), BF16),
        ],
        compiler_params=_params("parallel"),
        name="in_proj",
    )(x2, g, w, pos_col, fpat, sgn, gq, gkv, wq, wqr, wk, wv)


def _swa_kernel(q_ref, kp_ref, kc_ref, vp_ref, vc_ref, bias_ref, sink_ref, o_ref, *, nw):
    k_all = jnp.concatenate([kp_ref[0], kc_ref[0]], axis=0)
    v_all = jnp.concatenate([vp_ref[0], vc_ref[0]], axis=0)
    scale = SWA_HEAD_DIM ** -0.5
    for w in range(nw):
        rows = slice(w * WINDOW, (w + 1) * WINDOW)
        k = k_all[w * WINDOW:(w + 2) * WINDOW]
        v = v_all[w * WINDOW:(w + 2) * WINDOW]
        outs = []
        for h in range(SWA_Q_HEADS):
            hk = h // SWA_GROUP
            qh = q_ref[0, rows, h * SWA_HEAD_DIM:(h + 1) * SWA_HEAD_DIM]
            kh = k[:, hk * SWA_HEAD_DIM:(hk + 1) * SWA_HEAD_DIM]
            vh = v[:, hk * SWA_HEAD_DIM:(hk + 1) * SWA_HEAD_DIM]
            s = lax.dot_general(qh, kh, (((1,), (1,)), ((), ())), preferred_element_type=F32)
            s = s * scale + bias_ref[0, w, h]
            sink = sink_ref[h]
            m = jnp.maximum(jnp.max(s, axis=1, keepdims=True), sink)
            p = jnp.exp(s - m)
            denom = jnp.sum(p, axis=1, keepdims=True) + jnp.exp(sink - m)
            o = jnp.dot(p.astype(BF16), vh, preferred_element_type=F32)
            outs.append(o / denom)
        o_ref[0, rows, :] = jnp.concatenate(outs, axis=1).astype(o_ref.dtype)


def _swa(qkv, bias, sinks, nw=2):
    b, s, _ = qkv.shape
    nb = s // WINDOW
    kcol = (SWA_Q_HEADS * SWA_HEAD_DIM) // LANES
    vcol = kcol + 1
    prev = lambda n: jnp.maximum(n * nw - 1, 0)
    tq = nw * WINDOW
    return pl.pallas_call(
        functools.partial(_swa_kernel, nw=nw),
        grid=(b, nb // nw),
        in_specs=[
            pl.BlockSpec((1, tq, SWA_Q_HEADS * SWA_HEAD_DIM), lambda i, n: (i, n, 0)),
            pl.BlockSpec((1, WINDOW, LANES), lambda i, n: (i, prev(n), kcol)),
            pl.BlockSpec((1, tq, LANES), lambda i, n: (i, n, kcol)),
            pl.BlockSpec((1, WINDOW, LANES), lambda i, n: (i, prev(n), vcol)),
            pl.BlockSpec((1, tq, LANES), lambda i, n: (i, n, vcol)),
            pl.BlockSpec((1, nw, SWA_Q_HEADS, WINDOW, 2 * WINDOW), lambda i, n: (i, n, 0, 0, 0)),
            pl.BlockSpec(memory_space=pltpu.SMEM),
        ],
        out_specs=pl.BlockSpec((1, tq, BRANCH_WIDTH), lambda i, n: (i, n, 0)),
        out_shape=jax.ShapeDtypeStruct((b, s, BRANCH_WIDTH), BF16),
        compiler_params=_params("parallel", "parallel"),
        name="swa_attn",
    )(qkv, qkv, qkv, qkv, qkv, bias, sinks)


_CONV_HALO = 32
_CONV_CHUNK = 32


def _conv_kernel(uh_ref, uc_ref, w_ref, bdw_ref, g_ref, b_ref, o_ref, ext_ref, sh_ref, *, ts):
    c = CONV_CHANNELS

    def glu(u):
        return u[:, :c] * jax.nn.sigmoid(u[:, c:])

    halo = glu(uh_ref[0])
    ext_ref[0:_CONV_HALO, :] = jnp.where(pl.program_id(1) > 0, halo, 0.0)
    ext_ref[_CONV_HALO:, :] = glu(uc_ref[0])
    span = sh_ref.shape[1]
    for r in range(1, SUBLANES):
        sh_ref[r - 1] = ext_ref[r:r + span, :]
    lead = _CONV_HALO - (CONV_WIDTH - 1)
    for r0 in range(0, ts, _CONV_CHUNK):
        acc = jnp.broadcast_to(bdw_ref[...], (_CONV_CHUNK, c))
        for j in range(CONV_WIDTH):
            aligned, r = divmod(lead + j, SUBLANES)
            start = r0 + aligned * SUBLANES
            if r == 0:
                tap = ext_ref[start:start + _CONV_CHUNK, :]
            else:
                tap = sh_ref[r - 1, start:start + _CONV_CHUNK, :]
            acc = acc + w_ref[j:j + 1, :] * tap
        mu = jnp.mean(acc, axis=-1, keepdims=True)
        d = acc - mu
        var = jnp.mean(d * d, axis=-1, keepdims=True)
        y = d * lax.rsqrt(var + EPS) * g_ref[...] + b_ref[...]
        o_ref[0, r0:r0 + _CONV_CHUNK, :] = (y * jax.nn.sigmoid(y)).astype(o_ref.dtype)


def _conv(u, w_dw, b_dw, g_ln, b_ln, ts=512):
    b, s, _ = u.shape
    ratio = ts // _CONV_HALO
    const = lambda i, n: (0, 0)
    return pl.pallas_call(
        functools.partial(_conv_kernel, ts=ts),
        grid=(b, s // ts),
        in_specs=[
            pl.BlockSpec((1, _CONV_HALO, _U_W), lambda i, n: (i, jnp.maximum(n * ratio - 1, 0), 0)),
            pl.BlockSpec((1, ts, _U_W), lambda i, n: (i, n, 0)),
            pl.BlockSpec((CONV_WIDTH, CONV_CHANNELS), const),
            pl.BlockSpec((1, CONV_CHANNELS), const),
            pl.BlockSpec((1, CONV_CHANNELS), const),
            pl.BlockSpec((1, CONV_CHANNELS), const),
        ],
        out_specs=pl.BlockSpec((1, ts, CONV_CHANNELS), lambda i, n: (i, n, 0)),
        out_shape=jax.ShapeDtypeStruct((b, s, CONV_CHANNELS), BF16),
        scratch_shapes=[pltpu.VMEM((ts + _CONV_HALO, CONV_CHANNELS), F32),
                        pltpu.VMEM((SUBLANES - 1, ts + _CONV_HALO - SUBLANES, CONV_CHANNELS), F32)],
        compiler_params=_params("parallel", "parallel"),
        name="conformer_conv",
    )(u, u, w_dw, b_dw, g_ln, b_ln)


def _mla_prep(c, pos_ref, fpat_ref, sgn_ref, gq_ref, gkv_ref,
              wq_ref, wqr_ref, wk_ref, wv_ref, q_ref, k_ref, v_ref):
    cq = c[:, 0:MLA_Q_RANK]
    ckv = c[:, MLA_Q_RANK:MLA_Q_RANK + MLA_KV_RANK]
    kpe_a = c[:, MLA_Q_RANK + MLA_KV_RANK:MLA_Q_RANK + MLA_KV_RANK + LANES]
    kpe_b = c[:, MLA_Q_RANK + MLA_KV_RANK + LANES:]
    ang = pos_ref[...].astype(F32) * fpat_ref[...]
    sgn = sgn_ref[...]
    cos = jnp.where(sgn != 0.0, jnp.cos(ang), 1.0)
    sin = jnp.sin(ang) * sgn
    hq = _rms(cq, gq_ref[...]).astype(BF16)
    hkv = _rms(ckv, gkv_ref[...]).astype(BF16)
    k_rope = kpe_a * cos + kpe_b * sin
    q_scale = (MLA_NOPE_DIM + MLA_ROPE_DIM) ** -0.5 * math.log2(math.e)
    cos_q = cos * q_scale
    sin_q = sin * q_scale
    qf = jnp.dot(hq, wq_ref[...], preferred_element_type=F32)
    qr = jnp.dot(hq, wqr_ref[...], preferred_element_type=F32)
    kn = jnp.dot(hkv, wk_ref[...], preferred_element_type=F32)
    for h in range(MLA_HEADS):
        sl = slice(h * LANES, (h + 1) * LANES)
        q_ref[:, sl] = (qf[:, sl] * cos_q + qr[:, sl] * sin_q).astype(q_ref.dtype)
        k_ref[:, sl] = (kn[:, sl] + k_rope).astype(k_ref.dtype)
    half_block = lax.broadcasted_iota(jnp.int32, (1, v_ref.shape[1]), 1) // MLA_V_DIM
    ones = ((half_block % 4 == 1) | (half_block % 4 == 2)).astype(F32)
    v_ref[...] = (jnp.dot(hkv, wv_ref[...], preferred_element_type=F32) + ones).astype(v_ref.dtype)


def _mla_attn_kernel(q_ref, k_ref, v_ref, o_ref, sa_ref, sb_ref, m_ref, acc_ref, *, t, nh, nq):
    base = pl.program_id(2) * nq

    def init():
        m_ref[...] = jnp.full(m_ref.shape, NEG_INF, F32)
        acc_ref[...] = jnp.zeros(acc_ref.shape, F32)

    def scores(j, s_ref, qi):
        k0 = pl.multiple_of(j * t, t)
        for hh in range(nh):
            lanes = slice(hh * LANES, (hh + 1) * LANES)
            s_ref[hh] = lax.dot_general(q_ref[0, qi * t:(qi + 1) * t, lanes], k_ref[0, pl.ds(k0, t), lanes],
                                        (((1,), (1,)), ((), ())), preferred_element_type=F32)

    def consume(j, s_ref, diagonal):
        k0 = pl.multiple_of(j * t, t)
        for hh in range(nh):
            lanes = slice(hh * LANES, (hh + 1) * LANES)
            s = s_ref[hh]
            if diagonal:
                row = lax.broadcasted_iota(jnp.int32, (t, t), 0)
                col = lax.broadcasted_iota(jnp.int32, (t, t), 1)
                s = jnp.where(col <= row, s, NEG_INF)
            m_old = m_ref[hh]
            m_new = jnp.maximum(m_old, jnp.max(s, axis=1, keepdims=True))
            alpha = jnp.exp2(m_old - m_new)
            p = jnp.exp2(s - jnp.concatenate([m_new] * (t // LANES), axis=1))
            m_ref[hh] = m_new
            acc_ref[hh] = alpha * acc_ref[hh] + jnp.dot(
                p.astype(BF16), v_ref[0, pl.ds(k0, t), lanes], preferred_element_type=F32)

    def finish(qi):
        lane = lax.broadcasted_iota(jnp.int32, (t, LANES), 1)
        for pair in range(nh // 2):
            even, odd = acc_ref[2 * pair], acc_ref[2 * pair + 1]
            acc = jnp.where(lane < MLA_V_DIM, even, odd)
            den = jnp.where(lane < MLA_V_DIM, pltpu.roll(even, MLA_V_DIM, 1), pltpu.roll(odd, MLA_V_DIM, 1))
            o_ref[0, qi * t:(qi + 1) * t, pair * LANES:(pair + 1) * LANES] = (acc / den).astype(o_ref.dtype)

    def sweep(first, other, qi, n_pairs):
        def pair(jj, carry):
            j = 2 * jj
            scores(j + 1, other, qi)
            consume(j, first, diagonal=False)
            scores(j + 2, first, qi)
            consume(j + 1, other, diagonal=False)
            return carry

        lax.fori_loop(0, n_pairs, pair, 0)

    first, other = sa_ref, sb_ref
    scores(0, first, 0)
    for qi in range(nq):
        n = base + qi
        init()
        sweep(first, other, qi, n // 2)
        if qi % 2 == 0:
            if qi + 1 < nq:
                scores(0, other, qi + 1)
            consume(n, first, diagonal=True)
            first, other = other, first
        else:
            scores(n, other, qi)
            consume(n - 1, first, diagonal=False)
            if qi + 1 < nq:
                scores(0, first, qi + 1)
            consume(n, other, diagonal=True)
        finish(qi)


def _mla_attn(qf, kf, v, t=512, nh=4, nq=4):
    b, s, _ = qf.shape
    width = nh * LANES
    return pl.pallas_call(
        functools.partial(_mla_attn_kernel, t=t, nh=nh, nq=nq),
        grid=(b, MLA_HEADS // nh, s // (nq * t)),
        in_specs=[
            pl.BlockSpec((1, nq * t, width), lambda i, p, n: (i, n, p)),
            pl.BlockSpec((1, s, width), lambda i, p, n: (i, 0, p)),
            pl.BlockSpec((1, s, width), lambda i, p, n: (i, 0, p)),
        ],
        out_specs=pl.BlockSpec((1, nq * t, nh * MLA_V_DIM), lambda i, p, n: (i, n, p)),
        out_shape=jax.ShapeDtypeStruct((b, s, MLA_HEADS * MLA_V_DIM), BF16),
        scratch_shapes=[pltpu.VMEM((nh, t, t), F32), pltpu.VMEM((nh, t, t), F32),
                        pltpu.VMEM((nh, t, LANES), F32), pltpu.VMEM((nh, t, LANES), F32)],
        compiler_params=_params("parallel", "parallel", "arbitrary"),
        name="mla_attn",
    )(qf, kf, v)


def _merge_kernel(ya_ref, yb_ref, yc_ref, gate_ref, x_ref, wb_ref, wo_ref, o_ref):
    merged = None
    for n, y_ref in enumerate((ya_ref, yb_ref, yc_ref)):
        br = jnp.dot(y_ref[...], wb_ref[n], preferred_element_type=F32)
        term = jax.nn.sigmoid(gate_ref[:, n * D_MODEL:(n + 1) * D_MODEL].astype(F32)) * br
        merged = term if merged is None else merged + term
    o_ref[...] = x_ref[...] + jnp.dot(merged.astype(BF16), wo_ref[...], preferred_element_type=F32)


def _merge(ya, yb, yc, gates, x2, wb, wo, tm=512):
    t = x2.shape[0]
    row = lambda i: (i, 0)
    return pl.pallas_call(
        _merge_kernel,
        grid=(t // tm,),
        in_specs=[
            pl.BlockSpec((tm, BRANCH_WIDTH), row),
            pl.BlockSpec((tm, BRANCH_WIDTH), row),
            pl.BlockSpec((tm, BRANCH_WIDTH), row),
            pl.BlockSpec((tm, _GATE_W), row),
            pl.BlockSpec((tm, D_MODEL), row),
            pl.BlockSpec((N_BRANCHES, BRANCH_WIDTH, D_MODEL), lambda i: (0, 0, 0)),
            pl.BlockSpec((D_MODEL, D_MODEL), lambda i: (0, 0)),
        ],
        out_specs=pl.BlockSpec((tm, D_MODEL), row),
        out_shape=jax.ShapeDtypeStruct((t, D_MODEL), F32),
        compiler_params=_params("parallel"),
        name="merge_out",
    )(ya, yb, yc, gates, x2, wb, wo)


def _ffn_kernel(x_ref, g_ref, wu_ref, wd_ref, *rest, out_norm):
    if out_norm:
        gout_ref, o_ref, h_ref = rest
    else:
        o_ref, h_ref = rest
    f = pl.program_id(1)

    @pl.when(f == 0)
    def _():
        xf = x_ref[...]
        h_ref[...] = _rms(xf, g_ref[...]).astype(BF16)
        o_ref[...] = xf

    up = jnp.dot(h_ref[...], wu_ref[...], preferred_element_type=F32)
    act = jnp.square(jnp.maximum(up, 0.0)).astype(BF16)
    o_ref[...] += jnp.dot(act, wd_ref[...], preferred_element_type=F32)

    if out_norm:
        @pl.when(f == pl.num_programs(1) - 1)
        def _():
            o_ref[...] = _rms(o_ref[...], gout_ref[...])


def _ffn(x2, g, wu, wd, g_out=None, tm=1024, tf=2048):
    t = x2.shape[0]
    vec = pl.BlockSpec((1, D_MODEL), lambda i, f: (0, 0))
    in_specs = [
        pl.BlockSpec((tm, D_MODEL), lambda i, f: (i, 0)),
        vec,
        pl.BlockSpec((D_MODEL, tf), lambda i, f: (0, f)),
        pl.BlockSpec((tf, D_MODEL), lambda i, f: (f, 0)),
    ]
    args = [x2, g, wu, wd]
    if g_out is not None:
        in_specs.append(vec)
        args.append(g_out)
    return pl.pallas_call(
        functools.partial(_ffn_kernel, out_norm=g_out is not None),
        grid=(t // tm, D_FF // tf),
        in_specs=in_specs,
        out_specs=pl.BlockSpec((tm, D_MODEL), lambda i, f: (i, 0)),
        out_shape=jax.ShapeDtypeStruct((t, D_MODEL), F32),
        scratch_shapes=[pltpu.VMEM((tm, D_MODEL), BF16)],
        compiler_params=_params("parallel", "arbitrary"),
        name="ffn",
    )(*args)


def _pack_in_weights(w):
    q_end = _QKV_W
    u_end = q_end + _U_W
    cq_end = u_end + MLA_Q_RANK
    ckv_end = cq_end + MLA_KV_RANK
    kpe_end = ckv_end + MLA_ROPE_DIM
    half = MLA_ROPE_DIM // 2
    kpe = w[:, ckv_end:kpe_end]
    z_lo = jnp.zeros((w.shape[0], MLA_NOPE_DIM), w.dtype)
    z_hi = jnp.zeros((w.shape[0], LANES - MLA_NOPE_DIM - MLA_ROPE_DIM), w.dtype)
    kpe_a = jnp.concatenate([z_lo, kpe, z_hi], axis=1)
    kpe_b = jnp.concatenate([z_lo, kpe[:, half:], kpe[:, :half], z_hi], axis=1)
    return jnp.concatenate([w[:, :ckv_end], kpe_a, kpe_b, w[:, kpe_end:]], axis=1).astype(BF16)


def _pack_q_up(w):
    r = w.shape[0]
    half = MLA_ROPE_DIM // 2
    wh = w.reshape(r, MLA_HEADS, MLA_NOPE_DIM + MLA_ROPE_DIM)
    nope, pe = wh[..., :MLA_NOPE_DIM], wh[..., MLA_NOPE_DIM:]
    pad = jnp.zeros((r, MLA_HEADS, LANES - MLA_NOPE_DIM - MLA_ROPE_DIM), w.dtype)
    full = jnp.concatenate([nope, pe, pad], axis=-1)
    rot = jnp.concatenate([jnp.zeros_like(nope), pe[..., half:], pe[..., :half], pad], axis=-1)
    return (full.reshape(r, MLA_HEADS * LANES).astype(BF16),
            rot.reshape(r, MLA_HEADS * LANES).astype(BF16))


def _pack_kv_up(w):
    r = w.shape[0]
    wh = w.reshape(r, MLA_HEADS // 2, 2, MLA_NOPE_DIM + MLA_V_DIM)
    kn, v = wh[..., :MLA_NOPE_DIM], wh[..., MLA_NOPE_DIM:]
    zk = jnp.zeros_like(kn)
    k_full = jnp.concatenate([kn, zk], axis=-1)
    zv = jnp.zeros_like(v[:, :, 0])
    v_full = jnp.concatenate([v[:, :, 0], zv, zv, v[:, :, 1]], axis=-1)
    return (k_full.reshape(r, MLA_HEADS * LANES).astype(BF16),
            v_full.reshape(r, MLA_HEADS * LANES).astype(BF16))


def _rope_lane_patterns():
    half = MLA_ROPE_DIM // 2
    freqs = jnp.exp(-math.log(ROPE_THETA) * jnp.arange(half, dtype=F32) / half)
    zeros_lo = jnp.zeros((MLA_NOPE_DIM,), F32)
    zeros_hi = jnp.zeros((LANES - MLA_NOPE_DIM - MLA_ROPE_DIM,), F32)
    fpat = jnp.concatenate([zeros_lo, freqs, freqs, zeros_hi]).reshape(1, LANES)
    ones = jnp.ones((half,), F32)
    sgn = jnp.concatenate([zeros_lo, -ones, ones, zeros_hi]).reshape(1, LANES)
    return fpat, sgn


def kernel(x, positions, rel_bias, g_final, g_mix, w_in, swa_sinks, g_q_norm, w_q_up, g_kv_norm, w_kv_up, w_dw, b_dw, g_conv_ln, b_conv_ln, w_branch, w_out, g_mlp, w_up, w_down):
    b, s, d = x.shape
    t = b * s
    x2 = x.reshape(t, d)
    pos_col = positions.reshape(t, 1)
    fpat, sgn = _rope_lane_patterns()
    bias = _swa_bias(positions, rel_bias)
    for l in range(DEPTH):
        wq, wqr = _pack_q_up(w_q_up[l])
        wk, wv = _pack_kv_up(w_kv_up[l])
        qkv, u, gates, qf, kf, v = _in_proj(
            x2, g_mix[l].reshape(1, d), _pack_in_weights(w_in[l]), pos_col, fpat, sgn,
            g_q_norm[l].reshape(1, -1), g_kv_norm[l].reshape(1, -1), wq, wqr, wk, wv)
        y_a = _swa(qkv.reshape(b, s, _QKV_W), bias, swa_sinks[l])
        y_b = _conv(u.reshape(b, s, _U_W), w_dw[l].reshape(CONV_WIDTH, CONV_CHANNELS),
                    b_dw[l].reshape(1, -1), g_conv_ln[l].reshape(1, -1), b_conv_ln[l].reshape(1, -1))
        hw = MLA_HEADS * LANES
        y_c = _mla_attn(qf.reshape(b, s, hw), kf.reshape(b, s, hw),
                        v.reshape(b, s, hw))
        x2 = _merge(y_a.reshape(t, -1), y_b.reshape(t, -1), y_c.reshape(t, -1), gates, x2,
                    w_branch[l].astype(BF16), w_out[l].astype(BF16))
        x2 = _ffn(x2, g_mlp[l].reshape(1, d), w_up[l].astype(BF16), w_down[l].astype(BF16),
                  g_out=g_final.reshape(1, d) if l == DEPTH - 1 else None)
    return x2.reshape(b, s, d)
```

```python
import functools
import math

import jax
import jax.numpy as jnp
from jax import lax
from jax.experimental import pallas as pl
from jax.experimental.pallas import tpu as pltpu

D_MODEL = 1024
DEPTH = 4
BRANCH_WIDTH = 512
N_BRANCHES = 3
SWA_Q_HEADS = 8
SWA_KV_HEADS = 2
SWA_GROUP = SWA_Q_HEADS // SWA_KV_HEADS
SWA_HEAD_DIM = 64
WINDOW = 128
N_BUCKETS = 32
MAX_DISTANCE = 128
CONV_CHANNELS = 512
CONV_WIDTH = 31
MLA_HEADS = 8
MLA_Q_RANK = 256
MLA_KV_RANK = 128
MLA_NOPE_DIM = 64
MLA_ROPE_DIM = 32
MLA_V_DIM = 64
ROPE_THETA = 10000.0
D_FF = 4 * D_MODEL
EPS = 1e-6
NEG_INF = -1e30

LANES = 128
SUBLANES = 8
VMEM_LIMIT = 56 * 1024 * 1024

F32 = jnp.float32
BF16 = jnp.bfloat16

_MAX_EXACT = N_BUCKETS // 2
_T5_THRESHOLDS = tuple(range(1, _MAX_EXACT + 1)) + tuple(
    math.ceil(_MAX_EXACT * (MAX_DISTANCE / _MAX_EXACT) ** (j / (N_BUCKETS - _MAX_EXACT)))
    for j in range(1, N_BUCKETS - _MAX_EXACT))

_QKV_W = 768
_U_W = 2 * CONV_CHANNELS
_CMLA_W = MLA_Q_RANK + MLA_KV_RANK + 2 * LANES
_GATE_W = N_BRANCHES * D_MODEL
_IN_W = _QKV_W + _U_W + _CMLA_W + _GATE_W


def _params(*semantics):
    return pltpu.CompilerParams(dimension_semantics=semantics, vmem_limit_bytes=VMEM_LIMIT)


def _rms(xf, g):
    return xf * lax.rsqrt(jnp.mean(xf * xf, axis=-1, keepdims=True) + EPS) * g


def _bias_kernel(posq_ref, pkp_ref, pkc_ref, tbl_ref, out_ref, *, nw):
    first_step = pl.program_id(1) == 0
    pk_all = jnp.concatenate([jnp.where(first_step, 0, pkp_ref[0]), pkc_ref[0]], axis=1)
    ki = lax.broadcasted_iota(jnp.int32, (SUBLANES, 2 * WINDOW), 1)

    dist = lax.broadcasted_iota(jnp.int32, (SUBLANES, LANES), 1)
    by_dist = [tbl_ref[pl.ds(h, 1), 0:LANES] + jnp.zeros((SUBLANES, LANES), F32) for h in range(SWA_Q_HEADS)]
    for b in range(1, N_BUCKETS):
        ge = dist >= _T5_THRESHOLDS[b - 1]
        for h in range(SWA_Q_HEADS):
            by_dist[h] = jnp.where(ge, tbl_ref[pl.ds(b * SWA_Q_HEADS + h, 1), 0:LANES], by_dist[h])

    for w in range(nw):
        pk = pk_all[:, w * WINDOW:(w + 2) * WINDOW]
        has_prev = (ki >= 0) if w else (jnp.logical_not(first_step) | (ki >= WINDOW))

        def rows(r, carry, w=w, pk=pk, has_prev=has_prev):
            r0 = pl.multiple_of(r * SUBLANES, SUBLANES)
            pq = posq_ref[0, pl.ds(w * WINDOW + r0, SUBLANES), :]
            n = jnp.minimum(jnp.maximum(pq - pk, 0), LANES - 1)
            qi = WINDOW + r0 + lax.broadcasted_iota(jnp.int32, (SUBLANES, 2 * WINDOW), 0)
            mask = (ki <= qi) & (qi - ki < WINDOW) & has_prev
            for h in range(SWA_Q_HEADS):
                val = jnp.concatenate(
                    [jnp.take_along_axis(by_dist[h], n[:, half * LANES:(half + 1) * LANES], axis=1)
                     for half in range(2 * WINDOW // LANES)], axis=1)
                out_ref[0, w, h, pl.ds(r0, SUBLANES), :] = jnp.where(mask, val, NEG_INF)
            return carry

        lax.fori_loop(0, WINDOW // SUBLANES, rows, 0, unroll=8)


def _swa_bias(positions, rel_bias, nw=4):
    b, s = positions.shape
    nb = s // WINDOW
    tbl = jnp.broadcast_to(rel_bias.astype(F32).reshape(N_BUCKETS * SWA_Q_HEADS, 1),
                           (N_BUCKETS * SWA_Q_HEADS, 2 * WINDOW))
    pos_col = positions.reshape(b, s, 1)
    pos_row = positions.reshape(b, 1, s)
    return pl.pallas_call(
        functools.partial(_bias_kernel, nw=nw),
        grid=(b, nb // nw),
        in_specs=[
            pl.BlockSpec((1, nw * WINDOW, 1), lambda i, n: (i, n, 0)),
            pl.BlockSpec((1, 1, WINDOW), lambda i, n: (i, 0, jnp.maximum(n * nw - 1, 0))),
            pl.BlockSpec((1, 1, nw * WINDOW), lambda i, n: (i, 0, n)),
            pl.BlockSpec((N_BUCKETS * SWA_Q_HEADS, 2 * WINDOW), lambda i, n: (0, 0)),
        ],
        out_specs=pl.BlockSpec((1, nw, SWA_Q_HEADS, WINDOW, 2 * WINDOW), lambda i, n: (i, n, 0, 0, 0)),
        out_shape=jax.ShapeDtypeStruct((b, nb, SWA_Q_HEADS, WINDOW, 2 * WINDOW), F32),
        compiler_params=_params("parallel", "parallel"),
        name="swa_bias",
    )(pos_col, pos_row, pos_row, tbl)


def _in_proj_kernel(x_ref, g_ref, w_ref, pos_ref, fpat_ref, sgn_ref, gq_ref, gkv_ref,
                    wq_ref, wqr_ref, wk_ref, wv_ref,
                    qkv_ref, u_ref, gate_ref, q_ref, k_ref, v_ref, *, chunk):
    h = _rms(x_ref[...], g_ref[...]).astype(BF16)

    def project(col, width, out_ref):
        for c0 in range(0, width, chunk):
            c1 = min(c0 + chunk, width)
            out_ref[:, c0:c1] = jnp.dot(h, w_ref[:, col + c0:col + c1],
                                        preferred_element_type=F32).astype(out_ref.dtype)

    project(0, _QKV_W, qkv_ref)
    project(_QKV_W, _U_W, u_ref)
    c_col = _QKV_W + _U_W
    project(c_col + _CMLA_W, _GATE_W, gate_ref)
    c = jnp.dot(h, w_ref[:, c_col:c_col + _CMLA_W], preferred_element_type=F32)
    _mla_prep(c, pos_ref, fpat_ref, sgn_ref, gq_ref, gkv_ref, wq_ref, wqr_ref, wk_ref, wv_ref,
              q_ref, k_ref, v_ref)


def _in_proj(x2, g, w, pos_col, fpat, sgn, gq, gkv, wq, wqr, wk, wv, tm=512):
    t = x2.shape[0]
    row = lambda i: (i, 0)
    const = lambda i: (0, 0)
    hw = MLA_HEADS * LANES
    return pl.pallas_call(
        functools.partial(_in_proj_kernel, chunk=512),
        grid=(t // tm,),
        in_specs=[
            pl.BlockSpec((tm, D_MODEL), row),
            pl.BlockSpec((1, D_MODEL), const),
            pl.BlockSpec((D_MODEL, _IN_W), const),
            pl.BlockSpec((tm, 1), row),
            pl.BlockSpec((1, LANES), const),
            pl.BlockSpec((1, LANES), const),
            pl.BlockSpec((1, MLA_Q_RANK), const),
            pl.BlockSpec((1, MLA_KV_RANK), const),
            pl.BlockSpec((MLA_Q_RANK, hw), const),
            pl.BlockSpec((MLA_Q_RANK, hw), const),
            pl.BlockSpec((MLA_KV_RANK, hw), const),
            pl.BlockSpec((MLA_KV_RANK, hw), const),
        ],
        out_specs=[
            pl.BlockSpec((tm, _QKV_W), row),
            pl.BlockSpec((tm, _U_W), row),
            pl.BlockSpec((tm, _GATE_W), row),
            pl.BlockSpec((tm, hw), row),
            pl.BlockSpec((tm, hw), row),
            pl.BlockSpec((tm, hw), row),
        ],
        out_shape=[
            jax.ShapeDtypeStruct((t, _QKV_W), BF16),
            jax.ShapeDtypeStruct((t, _U_W), F32),
            jax.ShapeDtypeStruct((t, _GATE_W), BF16),
            jax.ShapeDtypeStruct((t, hw), BF16),
            jax.ShapeDtypeStruct((t, hw), BF16),
            jax.ShapeDtypeStruct((t, hw), BF16),
        ],
        compiler_params=_params("parallel"),
        name="in_proj",
    )(x2, g, w, pos_col, fpat, sgn, gq, gkv, wq, wqr, wk, wv)


def _swa_kernel(q_ref, kp_ref, kc_ref, vp_ref, vc_ref, bias_ref, sink_ref, o_ref, *, nw):
    k_all = jnp.concatenate([kp_ref[0], kc_ref[0]], axis=0)
    v_all = jnp.concatenate([vp_ref[0], vc_ref[0]], axis=0)
    scale = SWA_HEAD_DIM ** -0.5
    for w in range(nw):
        rows = slice(w * WINDOW, (w + 1) * WINDOW)
        k = k_all[w * WINDOW:(w + 2) * WINDOW]
        v = v_all[w * WINDOW:(w + 2) * WINDOW]
        outs = []
        for h in range(SWA_Q_HEADS):
            hk = h // SWA_GROUP
            qh = q_ref[0, rows, h * SWA_HEAD_DIM:(h + 1) * SWA_HEAD_DIM]
            kh = k[:, hk * SWA_HEAD_DIM:(hk + 1) * SWA_HEAD_DIM]
            vh = v[:, hk * SWA_HEAD_DIM:(hk + 1) * SWA_HEAD_DIM]
            s = lax.dot_general(qh, kh, (((1,), (1,)), ((), ())), preferred_element_type=F32)
            s = s * scale + bias_ref[0, w, h]
            sink = sink_ref[h]
            m = jnp.maximum(jnp.max(s, axis=1, keepdims=True), sink)
            p = jnp.exp(s - m)
            denom = jnp.sum(p, axis=1, keepdims=True) + jnp.exp(sink - m)
            o = jnp.dot(p.astype(BF16), vh, preferred_element_type=F32)
            outs.append(o / denom)
        o_ref[0, rows, :] = jnp.concatenate(outs, axis=1).astype(o_ref.dtype)


def _swa(qkv, bias, sinks, nw=2):
    b, s, _ = qkv.shape
    nb = s // WINDOW
    kcol = (SWA_Q_HEADS * SWA_HEAD_DIM) // LANES
    vcol = kcol + 1
    prev = lambda n: jnp.maximum(n * nw - 1, 0)
    tq = nw * WINDOW
    return pl.pallas_call(
        functools.partial(_swa_kernel, nw=nw),
        grid=(b, nb // nw),
        in_specs=[
            pl.BlockSpec((1, tq, SWA_Q_HEADS * SWA_HEAD_DIM), lambda i, n: (i, n, 0)),
            pl.BlockSpec((1, WINDOW, LANES), lambda i, n: (i, prev(n), kcol)),
            pl.BlockSpec((1, tq, LANES), lambda i, n: (i, n, kcol)),
            pl.BlockSpec((1, WINDOW, LANES), lambda i, n: (i, prev(n), vcol)),
            pl.BlockSpec((1, tq, LANES), lambda i, n: (i, n, vcol)),
            pl.BlockSpec((1, nw, SWA_Q_HEADS, WINDOW, 2 * WINDOW), lambda i, n: (i, n, 0, 0, 0)),
            pl.BlockSpec(memory_space=pltpu.SMEM),
        ],
        out_specs=pl.BlockSpec((1, tq, BRANCH_WIDTH), lambda i, n: (i, n, 0)),
        out_shape=jax.ShapeDtypeStruct((b, s, BRANCH_WIDTH), BF16),
        compiler_params=_params("parallel", "parallel"),
        name="swa_attn",
    )(qkv, qkv, qkv, qkv, qkv, bias, sinks)


_CONV_HALO = 32
_CONV_CHUNK = 32


def _conv_kernel(uh_ref, uc_ref, w_ref, bdw_ref, g_ref, b_ref, o_ref, ext_ref, sh_ref, *, ts):
    c = CONV_CHANNELS

    def glu(u):
        return u[:, :c] * jax.nn.sigmoid(u[:, c:])

    halo = glu(uh_ref[0])
    ext_ref[0:_CONV_HALO, :] = jnp.where(pl.program_id(1) > 0, halo, 0.0)
    ext_ref[_CONV_HALO:, :] = glu(uc_ref[0])
    span = sh_ref.shape[1]
    for r in range(1, SUBLANES):
        sh_ref[r - 1] = ext_ref[r:r + span, :]
    lead = _CONV_HALO - (CONV_WIDTH - 1)
    for r0 in range(0, ts, _CONV_CHUNK):
        acc = jnp.broadcast_to(bdw_ref[...], (_CONV_CHUNK, c))
        for j in range(CONV_WIDTH):
            aligned, r = divmod(lead + j, SUBLANES)
            start = r0 + aligned * SUBLANES
            if r == 0:
                tap = ext_ref[start:start + _CONV_CHUNK, :]
            else:
                tap = sh_ref[r - 1, start:start + _CONV_CHUNK, :]
            acc = acc + w_ref[j:j + 1, :] * tap
        mu = jnp.mean(acc, axis=-1, keepdims=True)
        d = acc - mu
        var = jnp.mean(d * d, axis=-1, keepdims=True)
        y = d * lax.rsqrt(var + EPS) * g_ref[...] + b_ref[...]
        o_ref[0, r0:r0 + _CONV_CHUNK, :] = (y * jax.nn.sigmoid(y)).astype(o_ref.dtype)


def _conv(u, w_dw, b_dw, g_ln, b_ln, ts=512):
    b, s, _ = u.shape
    ratio = ts // _CONV_HALO
    const = lambda i, n: (0, 0)
    return pl.pallas_call(
        functools.partial(_conv_kernel, ts=ts),
        grid=(b, s // ts),
        in_specs=[
            pl.BlockSpec((1, _CONV_HALO, _U_W), lambda i, n: (i, jnp.maximum(n * ratio - 1, 0), 0)),
            pl.BlockSpec((1, ts, _U_W), lambda i, n: (i, n, 0)),
            pl.BlockSpec((CONV_WIDTH, CONV_CHANNELS), const),
            pl.BlockSpec((1, CONV_CHANNELS), const),
            pl.BlockSpec((1, CONV_CHANNELS), const),
            pl.BlockSpec((1, CONV_CHANNELS), const),
        ],
        out_specs=pl.BlockSpec((1, ts, CONV_CHANNELS), lambda i, n: (i, n, 0)),
        out_shape=jax.ShapeDtypeStruct((b, s, CONV_CHANNELS), BF16),
        scratch_shapes=[pltpu.VMEM((ts + _CONV_HALO, CONV_CHANNELS), F32),
                        pltpu.VMEM((SUBLANES - 1, ts + _CONV_HALO - SUBLANES, CONV_CHANNELS), F32)],
        compiler_params=_params("parallel", "parallel"),
        name="conformer_conv",
    )(u, u, w_dw, b_dw, g_ln, b_ln)


def _mla_prep(c, pos_ref, fpat_ref, sgn_ref, gq_ref, gkv_ref,
              wq_ref, wqr_ref, wk_ref, wv_ref, q_ref, k_ref, v_ref):
    cq = c[:, 0:MLA_Q_RANK]
    ckv = c[:, MLA_Q_RANK:MLA_Q_RANK + MLA_KV_RANK]
    kpe_a = c[:, MLA_Q_RANK + MLA_KV_RANK:MLA_Q_RANK + MLA_KV_RANK + LANES]
    kpe_b = c[:, MLA_Q_RANK + MLA_KV_RANK + LANES:]
    ang = pos_ref[...].astype(F32) * fpat_ref[...]
    sgn = sgn_ref[...]
    cos = jnp.where(sgn != 0.0, jnp.cos(ang), 1.0)
    sin = jnp.sin(ang) * sgn
    hq = _rms(cq, gq_ref[...]).astype(BF16)
    hkv = _rms(ckv, gkv_ref[...]).astype(BF16)
    k_rope = kpe_a * cos + kpe_b * sin
    q_scale = (MLA_NOPE_DIM + MLA_ROPE_DIM) ** -0.5 * math.log2(math.e)
    cos_q = cos * q_scale
    sin_q = sin * q_scale
    qf = jnp.dot(hq, wq_ref[...], preferred_element_type=F32)
    qr = jnp.dot(hq, wqr_ref[...], preferred_element_type=F32)
    kn = jnp.dot(hkv, wk_ref[...], preferred_element_type=F32)
    for h in range(MLA_HEADS):
        sl = slice(h * LANES, (h + 1) * LANES)
        q_ref[:, sl] = (qf[:, sl] * cos_q + qr[:, sl] * sin_q).astype(q_ref.dtype)
        k_ref[:, sl] = (kn[:, sl] + k_rope).astype(k_ref.dtype)
    half_block = lax.broadcasted_iota(jnp.int32, (1, v_ref.shape[1]), 1) // MLA_V_DIM
    ones = ((half_block % 4 == 1) | (half_block % 4 == 2)).astype(F32)
    v_ref[...] = (jnp.dot(hkv, wv_ref[...], preferred_element_type=F32) + ones).astype(v_ref.dtype)


def _mla_attn_kernel(q_ref, k_ref, v_ref, o_ref, sa_ref, sb_ref, m_ref, acc_ref, *, t, nh, nq):
    base = pl.program_id(2) * nq

    def init():
        m_ref[...] = jnp.full(m_ref.shape, NEG_INF, F32)
        acc_ref[...] = jnp.zeros(acc_ref.shape, F32)

    def scores(j, s_ref, qi):
        k0 = pl.multiple_of(j * t, t)
        for hh in range(nh):
            lanes = slice(hh * LANES, (hh + 1) * LANES)
            s_ref[hh] = lax.dot_general(q_ref[0, qi * t:(qi + 1) * t, lanes], k_ref[0, pl.ds(k0, t), lanes],
                                        (((1,), (1,)), ((), ())), preferred_element_type=F32)

    def consume(j, s_ref, diagonal):
        k0 = pl.multiple_of(j * t, t)
        half = t // 2
        blocks = [(0, half, half), (half, t, t)] if diagonal else [(0, t, t)]
        for hh in range(nh):
            lanes = slice(hh * LANES, (hh + 1) * LANES)
            for r0, r1, nk in blocks:
                s = s_ref[hh, r0:r1, 0:nk]
                if diagonal:
                    row = r0 + lax.broadcasted_iota(jnp.int32, (r1 - r0, nk), 0)
                    col = lax.broadcasted_iota(jnp.int32, (r1 - r0, nk), 1)
                    s = jnp.where(col <= row, s, NEG_INF)
                m_old = m_ref[hh, r0:r1, :]
                m_new = jnp.maximum(m_old, jnp.max(s, axis=1, keepdims=True))
                alpha = jnp.exp2(m_old - m_new)
                p = jnp.exp2(s - jnp.concatenate([m_new] * (nk // LANES), axis=1))
                m_ref[hh, r0:r1, :] = m_new
                acc_ref[hh, r0:r1, :] = alpha * acc_ref[hh, r0:r1, :] + jnp.dot(
                    p.astype(BF16), v_ref[0, pl.ds(k0, nk), lanes], preferred_element_type=F32)

    def finish(qi):
        lane = lax.broadcasted_iota(jnp.int32, (t, LANES), 1)
        for pair in range(nh // 2):
            even, odd = acc_ref[2 * pair], acc_ref[2 * pair + 1]
            acc = jnp.where(lane < MLA_V_DIM, even, odd)
            den = jnp.where(lane < MLA_V_DIM, pltpu.roll(even, MLA_V_DIM, 1), pltpu.roll(odd, MLA_V_DIM, 1))
            o_ref[0, qi * t:(qi + 1) * t, pair * LANES:(pair + 1) * LANES] = (acc / den).astype(o_ref.dtype)

    def sweep(first, other, qi, n_pairs):
        def pair(jj, carry):
            j = 2 * jj
            scores(j + 1, other, qi)
            consume(j, first, diagonal=False)
            scores(j + 2, first, qi)
            consume(j + 1, other, diagonal=False)
            return carry

        lax.fori_loop(0, n_pairs, pair, 0)

    first, other = sa_ref, sb_ref
    scores(0, first, 0)
    for qi in range(nq):
        n = base + qi
        init()
        sweep(first, other, qi, n // 2)
        if qi % 2 == 0:
            if qi + 1 < nq:
                scores(0, other, qi + 1)
            consume(n, first, diagonal=True)
            first, other = other, first
        else:
            scores(n, other, qi)
            consume(n - 1, first, diagonal=False)
            if qi + 1 < nq:
                scores(0, first, qi + 1)
            consume(n, other, diagonal=True)
        finish(qi)


def _mla_attn(qf, kf, v, t=512, nh=4, nq=4):
    b, s, _ = qf.shape
    width = nh * LANES
    return pl.pallas_call(
        functools.partial(_mla_attn_kernel, t=t, nh=nh, nq=nq),
        grid=(b, MLA_HEADS // nh, s // (nq * t)),
        in_specs=[
            pl.BlockSpec((1, nq * t, width), lambda i, p, n: (i, n, p)),
            pl.BlockSpec((1, s, width), lambda i, p, n: (i, 0, p)),
            pl.BlockSpec((1, s, width), lambda i, p, n: (i, 0, p)),
        ],
        out_specs=pl.BlockSpec((1, nq * t, nh * MLA_V_DIM), lambda i, p, n: (i, n, p)),
        out_shape=jax.ShapeDtypeStruct((b, s, MLA_HEADS * MLA_V_DIM), BF16),
        scratch_shapes=[pltpu.VMEM((nh, t, t), F32), pltpu.VMEM((nh, t, t), F32),
                        pltpu.VMEM((nh, t, LANES), F32), pltpu.VMEM((nh, t, LANES), F32)],
        compiler_params=_params("parallel", "parallel", "arbitrary"),
        name="mla_attn",
    )(qf, kf, v)


def _merge_kernel(ya_ref, yb_ref, yc_ref, gate_ref, x_ref, wb_ref, wo_ref, o_ref):
    merged = None
    for n, y_ref in enumerate((ya_ref, yb_ref, yc_ref)):
        br = jnp.dot(y_ref[...], wb_ref[n], preferred_element_type=F32)
        term = jax.nn.sigmoid(gate_ref[:, n * D_MODEL:(n + 1) * D_MODEL].astype(F32)) * br
        merged = term if merged is None else merged + term
    o_ref[...] = x_ref[...] + jnp.dot(merged.astype(BF16), wo_ref[...], preferred_element_type=F32)


def _merge(ya, yb, yc, gates, x2, wb, wo, tm=512):
    t = x2.shape[0]
    row = lambda i: (i, 0)
    return pl.pallas_call(
        _merge_kernel,
        grid=(t // tm,),
        in_specs=[
            pl.BlockSpec((tm, BRANCH_WIDTH), row),
            pl.BlockSpec((tm, BRANCH_WIDTH), row),
            pl.BlockSpec((tm, BRANCH_WIDTH), row),
            pl.BlockSpec((tm, _GATE_W), row),
            pl.BlockSpec((tm, D_MODEL), row),
            pl.BlockSpec((N_BRANCHES, BRANCH_WIDTH, D_MODEL), lambda i: (0, 0, 0)),
            pl.BlockSpec((D_MODEL, D_MODEL), lambda i: (0, 0)),
        ],
        out_specs=pl.BlockSpec((tm, D_MODEL), row),
        out_shape=jax.ShapeDtypeStruct((t, D_MODEL), F32),
        compiler_params=_params("parallel"),
        name="merge_out",
    )(ya, yb, yc, gates, x2, wb, wo)


def _ffn_kernel(x_ref, g_ref, wu_ref, wd_ref, *rest, out_norm):
    if out_norm:
        gout_ref, o_ref, h_ref = rest
    else:
        o_ref, h_ref = rest
    f = pl.program_id(1)

    @pl.when(f == 0)
    def _():
        xf = x_ref[...]
        h_ref[...] = _rms(xf, g_ref[...]).astype(BF16)
        o_ref[...] = xf

    up = jnp.dot(h_ref[...], wu_ref[...], preferred_element_type=F32)
    act = jnp.square(jnp.maximum(up, 0.0)).astype(BF16)
    o_ref[...] += jnp.dot(act, wd_ref[...], preferred_element_type=F32)

    if out_norm:
        @pl.when(f == pl.num_programs(1) - 1)
        def _():
            o_ref[...] = _rms(o_ref[...], gout_ref[...])


def _ffn(x2, g, wu, wd, g_out=None, tm=1024, tf=2048):
    t = x2.shape[0]
    vec = pl.BlockSpec((1, D_MODEL), lambda i, f: (0, 0))
    in_specs = [
        pl.BlockSpec((tm, D_MODEL), lambda i, f: (i, 0)),
        vec,
        pl.BlockSpec((D_MODEL, tf), lambda i, f: (0, f)),
        pl.BlockSpec((tf, D_MODEL), lambda i, f: (f, 0)),
    ]
    args = [x2, g, wu, wd]
    if g_out is not None:
        in_specs.append(vec)
        args.append(g_out)
    return pl.pallas_call(
        functools.partial(_ffn_kernel, out_norm=g_out is not None),
        grid=(t // tm, D_FF // tf),
        in_specs=in_specs,
        out_specs=pl.BlockSpec((tm, D_MODEL), lambda i, f: (i, 0)),
        out_shape=jax.ShapeDtypeStruct((t, D_MODEL), F32),
        scratch_shapes=[pltpu.VMEM((tm, D_MODEL), BF16)],
        compiler_params=_params("parallel", "arbitrary"),
        name="ffn",
    )(*args)


def _pack_in_weights(w):
    q_end = _QKV_W
    u_end = q_end + _U_W
    cq_end = u_end + MLA_Q_RANK
    ckv_end = cq_end + MLA_KV_RANK
    kpe_end = ckv_end + MLA_ROPE_DIM
    half = MLA_ROPE_DIM // 2
    kpe = w[:, ckv_end:kpe_end]
    z_lo = jnp.zeros((w.shape[0], MLA_NOPE_DIM), w.dtype)
    z_hi = jnp.zeros((w.shape[0], LANES - MLA_NOPE_DIM - MLA_ROPE_DIM), w.dtype)
    kpe_a = jnp.concatenate([z_lo, kpe, z_hi], axis=1)
    kpe_b = jnp.concatenate([z_lo, kpe[:, half:], kpe[:, :half], z_hi], axis=1)
    return jnp.concatenate([w[:, :ckv_end], kpe_a, kpe_b, w[:, kpe_end:]], axis=1).astype(BF16)


def _pack_q_up(w):
    r = w.shape[0]
    half = MLA_ROPE_DIM // 2
    wh = w.reshape(r, MLA_HEADS, MLA_NOPE_DIM + MLA_ROPE_DIM)
    nope, pe = wh[..., :MLA_NOPE_DIM], wh[..., MLA_NOPE_DIM:]
    pad = jnp.zeros((r, MLA_HEADS, LANES - MLA_NOPE_DIM - MLA_ROPE_DIM), w.dtype)
    full = jnp.concatenate([nope, pe, pad], axis=-1)
    rot = jnp.concatenate([jnp.zeros_like(nope), pe[..., half:], pe[..., :half], pad], axis=-1)
    return (full.reshape(r, MLA_HEADS * LANES).astype(BF16),
            rot.reshape(r, MLA_HEADS * LANES).astype(BF16))


def _pack_kv_up(w):
    r = w.shape[0]
    wh = w.reshape(r, MLA_HEADS // 2, 2, MLA_NOPE_DIM + MLA_V_DIM)
    kn, v = wh[..., :MLA_NOPE_DIM], wh[..., MLA_NOPE_DIM:]
    zk = jnp.zeros_like(kn)
    k_full = jnp.concatenate([kn, zk], axis=-1)
    zv = jnp.zeros_like(v[:, :, 0])
    v_full = jnp.concatenate([v[:, :, 0], zv, zv, v[:, :, 1]], axis=-1)
    return (k_full.reshape(r, MLA_HEADS * LANES).astype(BF16),
            v_full.reshape(r, MLA_HEADS * LANES).astype(BF16))


def _rope_lane_patterns():
    half = MLA_ROPE_DIM // 2
    freqs = jnp.exp(-math.log(ROPE_THETA) * jnp.arange(half, dtype=F32) / half)
    zeros_lo = jnp.zeros((MLA_NOPE_DIM,), F32)
    zeros_hi = jnp.zeros((LANES - MLA_NOPE_DIM - MLA_ROPE_DIM,), F32)
    fpat = jnp.concatenate([zeros_lo, freqs, freqs, zeros_hi]).reshape(1, LANES)
    ones = jnp.ones((half,), F32)
    sgn = jnp.concatenate([zeros_lo, -ones, ones, zeros_hi]).reshape(1, LANES)
    return fpat, sgn


def kernel(x, positions, rel_bias, g_final, g_mix, w_in, swa_sinks, g_q_norm, w_q_up, g_kv_norm, w_kv_up, w_dw, b_dw, g_conv_ln, b_conv_ln, w_branch, w_out, g_mlp, w_up, w_down):
    b, s, d = x.shape
    t = b * s
    x2 = x.reshape(t, d)
    pos_col = positions.reshape(t, 1)
    fpat, sgn = _rope_lane_patterns()
    bias = _swa_bias(positions, rel_bias)
    for l in range(DEPTH):
        wq, wqr = _pack_q_up(w_q_up[l])
        wk, wv = _pack_kv_up(w_kv_up[l])
        qkv, u, gates, qf, kf, v = _in_proj(
            x2, g_mix[l].reshape(1, d), _pack_in_weights(w_in[l]), pos_col, fpat, sgn,
            g_q_norm[l].reshape(1, -1), g_kv_norm[l].reshape(1, -1), wq, wqr, wk, wv)
        y_a = _swa(qkv.reshape(b, s, _QKV_W), bias, swa_sinks[l])
        y_b = _conv(u.reshape(b, s, _U_W), w_dw[l].reshape(CONV_WIDTH, CONV_CHANNELS),
                    b_dw[l].reshape(1, -1), g_conv_ln[l].reshape(1, -1), b_conv_ln[l].reshape(1, -1))
        hw = MLA_HEADS * LANES
        y_c = _mla_attn(qf.reshape(b, s, hw), kf.reshape(b, s, hw),
                        v.reshape(b, s, hw))
        x2 = _merge(y_a.reshape(t, -1), y_b.reshape(t, -1), y_c.reshape(t, -1), gates, x2,
                    w_branch[l].astype(BF16), w_out[l].astype(BF16))
        x2 = _ffn(x2, g_mlp[l].reshape(1, d), w_up[l].astype(BF16), w_down[l].astype(BF16),
                  g_out=g_final.reshape(1, d) if l == DEPTH - 1 else None)
    return x2.reshape(b, s, d)
```

```python
import functools
import math

import jax
import jax.numpy as jnp
from jax import lax
from jax.experimental import pallas as pl
from jax.experimental.pallas import tpu as pltpu

D_MODEL = 1024
DEPTH = 4
BRANCH_WIDTH = 512
N_BRANCHES = 3
SWA_Q_HEADS = 8
SWA_KV_HEADS = 2
SWA_GROUP = SWA_Q_HEADS // SWA_KV_HEADS
SWA_HEAD_DIM = 64
WINDOW = 128
N_BUCKETS = 32
MAX_DISTANCE = 128
CONV_CHANNELS = 512
CONV_WIDTH = 31
MLA_HEADS = 8
MLA_Q_RANK = 256
MLA_KV_RANK = 128
MLA_NOPE_DIM = 64
MLA_ROPE_DIM = 32
MLA_V_DIM = 64
ROPE_THETA = 10000.0
D_FF = 4 * D_MODEL
EPS = 1e-6
NEG_INF = -1e30

LANES = 128
SUBLANES = 8
VMEM_LIMIT = 56 * 1024 * 1024

F32 = jnp.float32
BF16 = jnp.bfloat16

_MAX_EXACT = N_BUCKETS // 2
_T5_THRESHOLDS = tuple(range(1, _MAX_EXACT + 1)) + tuple(
    math.ceil(_MAX_EXACT * (MAX_DISTANCE / _MAX_EXACT) ** (j / (N_BUCKETS - _MAX_EXACT)))
    for j in range(1, N_BUCKETS - _MAX_EXACT))

_QKV_W = 768
_U_W = 2 * CONV_CHANNELS
_CMLA_W = MLA_Q_RANK + MLA_KV_RANK + 2 * LANES
_GATE_W = N_BRANCHES * D_MODEL
_IN_W = _QKV_W + _U_W + _CMLA_W + _GATE_W


def _params(*semantics):
    return pltpu.CompilerParams(dimension_semantics=semantics, vmem_limit_bytes=VMEM_LIMIT)


def _rms(xf, g):
    return xf * lax.rsqrt(jnp.mean(xf * xf, axis=-1, keepdims=True) + EPS) * g


def _bias_kernel(posq_ref, pkp_ref, pkc_ref, tbl_ref, out_ref, *, nw):
    first_step = pl.program_id(1) == 0
    pk_all = jnp.concatenate([jnp.where(first_step, 0, pkp_ref[0]), pkc_ref[0]], axis=1)
    ki = lax.broadcasted_iota(jnp.int32, (SUBLANES, 2 * WINDOW), 1)

    dist = lax.broadcasted_iota(jnp.int32, (SUBLANES, LANES), 1)
    by_dist = [tbl_ref[pl.ds(h, 1), 0:LANES] + jnp.zeros((SUBLANES, LANES), F32) for h in range(SWA_Q_HEADS)]
    for b in range(1, N_BUCKETS):
        ge = dist >= _T5_THRESHOLDS[b - 1]
        for h in range(SWA_Q_HEADS):
            by_dist[h] = jnp.where(ge, tbl_ref[pl.ds(b * SWA_Q_HEADS + h, 1), 0:LANES], by_dist[h])

    for w in range(nw):
        pk = pk_all[:, w * WINDOW:(w + 2) * WINDOW]
        has_prev = (ki >= 0) if w else (jnp.logical_not(first_step) | (ki >= WINDOW))

        def rows(r, carry, w=w, pk=pk, has_prev=has_prev):
            r0 = pl.multiple_of(r * SUBLANES, SUBLANES)
            pq = posq_ref[0, pl.ds(w * WINDOW + r0, SUBLANES), :]
            n = jnp.minimum(jnp.maximum(pq - pk, 0), LANES - 1)
            qi = WINDOW + r0 + lax.broadcasted_iota(jnp.int32, (SUBLANES, 2 * WINDOW), 0)
            mask = (ki <= qi) & (qi - ki < WINDOW) & has_prev
            for h in range(SWA_Q_HEADS):
                val = jnp.concatenate(
                    [jnp.take_along_axis(by_dist[h], n[:, half * LANES:(half + 1) * LANES], axis=1)
                     for half in range(2 * WINDOW // LANES)], axis=1)
                out_ref[0, w, h, pl.ds(r0, SUBLANES), :] = jnp.where(mask, val, NEG_INF)
            return carry

        lax.fori_loop(0, WINDOW // SUBLANES, rows, 0, unroll=8)


def _swa_bias(positions, rel_bias, nw=4):
    b, s = positions.shape
    nb = s // WINDOW
    tbl = jnp.broadcast_to(rel_bias.astype(F32).reshape(N_BUCKETS * SWA_Q_HEADS, 1),
                           (N_BUCKETS * SWA_Q_HEADS, 2 * WINDOW))
    pos_col = positions.reshape(b, s, 1)
    pos_row = positions.reshape(b, 1, s)
    return pl.pallas_call(
        functools.partial(_bias_kernel, nw=nw),
        grid=(b, nb // nw),
        in_specs=[
            pl.BlockSpec((1, nw * WINDOW, 1), lambda i, n: (i, n, 0)),
            pl.BlockSpec((1, 1, WINDOW), lambda i, n: (i, 0, jnp.maximum(n * nw - 1, 0))),
            pl.BlockSpec((1, 1, nw * WINDOW), lambda i, n: (i, 0, n)),
            pl.BlockSpec((N_BUCKETS * SWA_Q_HEADS, 2 * WINDOW), lambda i, n: (0, 0)),
        ],
        out_specs=pl.BlockSpec((1, nw, SWA_Q_HEADS, WINDOW, 2 * WINDOW), lambda i, n: (i, n, 0, 0, 0)),
        out_shape=jax.ShapeDtypeStruct((b, nb, SWA_Q_HEADS, WINDOW, 2 * WINDOW), F32),
        compiler_params=_params("parallel", "parallel"),
        name="swa_bias",
    )(pos_col, pos_row, pos_row, tbl)


def _in_proj_kernel(x_ref, g_ref, w_ref, pos_ref, fpat_ref, sgn_ref, gq_ref, gkv_ref,
                    wq_ref, wqr_ref, wk_ref, wv_ref,
                    qkv_ref, u_ref, gate_ref, q_ref, k_ref, v_ref, *, chunk):
    h = _rms(x_ref[...], g_ref[...]).astype(BF16)

    def project(col, width, out_ref):
        for c0 in range(0, width, chunk):
            c1 = min(c0 + chunk, width)
            out_ref[:, c0:c1] = jnp.dot(h, w_ref[:, col + c0:col + c1],
                                        preferred_element_type=F32).astype(out_ref.dtype)

    project(0, _QKV_W, qkv_ref)
    project(_QKV_W, _U_W, u_ref)
    c_col = _QKV_W + _U_W
    project(c_col + _CMLA_W, _GATE_W, gate_ref)
    c = jnp.dot(h, w_ref[:, c_col:c_col + _CMLA_W], preferred_element_type=F32)
    _mla_prep(c, pos_ref, fpat_ref, sgn_ref, gq_ref, gkv_ref, wq_ref, wqr_ref, wk_ref, wv_ref,
              q_ref, k_ref, v_ref)


def _in_proj(x2, g, w, pos_col, fpat, sgn, gq, gkv, wq, wqr, wk, wv, tm=512):
    t = x2.shape[0]
    row = lambda i: (i, 0)
    const = lambda i: (0, 0)
    hw = MLA_HEADS * LANES
    return pl.pallas_call(
        functools.partial(_in_proj_kernel, chunk=512),
        grid=(t // tm,),
        in_specs=[
            pl.BlockSpec((tm, D_MODEL), row),
            pl.BlockSpec((1, D_MODEL), const),
            pl.BlockSpec((D_MODEL, _IN_W), const),
            pl.BlockSpec((tm, 1), row),
            pl.BlockSpec((1, LANES), const),
            pl.BlockSpec((1, LANES), const),
            pl.BlockSpec((1, MLA_Q_RANK), const),
            pl.BlockSpec((1, MLA_KV_RANK), const),
            pl.BlockSpec((MLA_Q_RANK, hw), const),
            pl.BlockSpec((MLA_Q_RANK, hw), const),
            pl.BlockSpec((MLA_KV_RANK, hw), const),
            pl.BlockSpec((MLA_KV_RANK, hw), const),
        ],
        out_specs=[
            pl.BlockSpec((tm, _QKV_W), row),
            pl.BlockSpec((tm, _U_W), row),
            pl.BlockSpec((tm, _GATE_W), row),
            pl.BlockSpec((tm, hw), row),
            pl.BlockSpec((tm, hw), row),
            pl.BlockSpec((tm, hw), row),
        ],
        out_shape=[
            jax.ShapeDtypeStruct((t, _QKV_W), BF16),
            jax.ShapeDtypeStruct((t, _U_W), F32),
            jax.ShapeDtypeStruct((t, _GATE_W), BF16),
            jax.ShapeDtypeStruct((t, hw), BF16),
            jax.ShapeDtypeStruct((t, hw), BF16),
            jax.ShapeDtypeStruct((t, hw), BF16),
        ],
        compiler_params=_params("parallel"),
        name="in_proj",
    )(x2, g, w, pos_col, fpat, sgn, gq, gkv, wq, wqr, wk, wv)


def _swa_kernel(q_ref, kp_ref, kc_ref, vp_ref, vc_ref, bias_ref, sink_ref, o_ref, *, nw):
    k_all = jnp.concatenate([kp_ref[0], kc_ref[0]], axis=0)
    v_all = jnp.concatenate([vp_ref[0], vc_ref[0]], axis=0)
    scale = SWA_HEAD_DIM ** -0.5
    for w in range(nw):
        rows = slice(w * WINDOW, (w + 1) * WINDOW)
        k = k_all[w * WINDOW:(w + 2) * WINDOW]
        v = v_all[w * WINDOW:(w + 2) * WINDOW]
        outs = []
        for h in range(SWA_Q_HEADS):
            hk = h // SWA_GROUP
            qh = q_ref[0, rows, h * SWA_HEAD_DIM:(h + 1) * SWA_HEAD_DIM]
            kh = k[:, hk * SWA_HEAD_DIM:(hk + 1) * SWA_HEAD_DIM]
            vh = v[:, hk * SWA_HEAD_DIM:(hk + 1) * SWA_HEAD_DIM]
            s = lax.dot_general(qh, kh, (((1,), (1,)), ((), ())), preferred_element_type=F32)
            s = s * scale + bias_ref[0, w, h]
            sink = sink_ref[h]
            m = jnp.maximum(jnp.max(s, axis=1, keepdims=True), sink)
            p = jnp.exp(s - m)
            denom = jnp.sum(p, axis=1, keepdims=True) + jnp.exp(sink - m)
            o = jnp.dot(p.astype(BF16), vh, preferred_element_type=F32)
            outs.append(o / denom)
        o_ref[0, rows, :] = jnp.concatenate(outs, axis=1).astype(o_ref.dtype)


def _swa(qkv, bias, sinks, nw=2):
    b, s, _ = qkv.shape
    nb = s // WINDOW
    kcol = (SWA_Q_HEADS * SWA_HEAD_DIM) // LANES
    vcol = kcol + 1
    prev = lambda n: jnp.maximum(n * nw - 1, 0)
    tq = nw * WINDOW
    return pl.pallas_call(
        functools.partial(_swa_kernel, nw=nw),
        grid=(b, nb // nw),
        in_specs=[
            pl.BlockSpec((1, tq, SWA_Q_HEADS * SWA_HEAD_DIM), lambda i, n: (i, n, 0)),
            pl.BlockSpec((1, WINDOW, LANES), lambda i, n: (i, prev(n), kcol)),
            pl.BlockSpec((1, tq, LANES), lambda i, n: (i, n, kcol)),
            pl.BlockSpec((1, WINDOW, LANES), lambda i, n: (i, prev(n), vcol)),
            pl.BlockSpec((1, tq, LANES), lambda i, n: (i, n, vcol)),
            pl.BlockSpec((1, nw, SWA_Q_HEADS, WINDOW, 2 * WINDOW), lambda i, n: (i, n, 0, 0, 0)),
            pl.BlockSpec(memory_space=pltpu.SMEM),
        ],
        out_specs=pl.BlockSpec((1, tq, BRANCH_WIDTH), lambda i, n: (i, n, 0)),
        out_shape=jax.ShapeDtypeStruct((b, s, BRANCH_WIDTH), BF16),
        compiler_params=_params("parallel", "parallel"),
        name="swa_attn",
    )(qkv, qkv, qkv, qkv, qkv, bias, sinks)


_CONV_HALO = 32
_CONV_CHUNK = 32


def _conv_kernel(uh_ref, uc_ref, w_ref, bdw_ref, g_ref, b_ref, o_ref, ext_ref, sh_ref, *, ts):
    c = CONV_CHANNELS

    def glu(u):
        return u[:, :c] * jax.nn.sigmoid(u[:, c:])

    halo = glu(uh_ref[0])
    ext_ref[0:_CONV_HALO, :] = jnp.where(pl.program_id(1) > 0, halo, 0.0)
    ext_ref[_CONV_HALO:, :] = glu(uc_ref[0])
    span = sh_ref.shape[1]
    for r in range(1, SUBLANES):
        sh_ref[r - 1] = ext_ref[r:r + span, :]
    lead = _CONV_HALO - (CONV_WIDTH - 1)
    for r0 in range(0, ts, _CONV_CHUNK):
        acc = jnp.broadcast_to(bdw_ref[...], (_CONV_CHUNK, c))
        for j in range(CONV_WIDTH):
            aligned, r = divmod(lead + j, SUBLANES)
            start = r0 + aligned * SUBLANES
            if r == 0:
                tap = ext_ref[start:start + _CONV_CHUNK, :]
            else:
                tap = sh_ref[r - 1, start:start + _CONV_CHUNK, :]
            acc = acc + w_ref[j:j + 1, :] * tap
        mu = jnp.mean(acc, axis=-1, keepdims=True)
        d = acc - mu
        var = jnp.mean(d * d, axis=-1, keepdims=True)
        y = d * lax.rsqrt(var + EPS) * g_ref[...] + b_ref[...]
        o_ref[0, r0:r0 + _CONV_CHUNK, :] = (y * jax.nn.sigmoid(y)).astype(o_ref.dtype)


def _conv(u, w_dw, b_dw, g_ln, b_ln, ts=512):
    b, s, _ = u.shape
    ratio = ts // _CONV_HALO
    const = lambda i, n: (0, 0)
    return pl.pallas_call(
        functools.partial(_conv_kernel, ts=ts),
        grid=(b, s // ts),
        in_specs=[
            pl.BlockSpec((1, _CONV_HALO, _U_W), lambda i, n: (i, jnp.maximum(n * ratio - 1, 0), 0)),
            pl.BlockSpec((1, ts, _U_W), lambda i, n: (i, n, 0)),
            pl.BlockSpec((CONV_WIDTH, CONV_CHANNELS), const),
            pl.BlockSpec((1, CONV_CHANNELS), const),
            pl.BlockSpec((1, CONV_CHANNELS), const),
            pl.BlockSpec((1, CONV_CHANNELS), const),
        ],
        out_specs=pl.BlockSpec((1, ts, CONV_CHANNELS), lambda i, n: (i, n, 0)),
        out_shape=jax.ShapeDtypeStruct((b, s, CONV_CHANNELS), BF16),
        scratch_shapes=[pltpu.VMEM((ts + _CONV_HALO, CONV_CHANNELS), F32),
                        pltpu.VMEM((SUBLANES - 1, ts + _CONV_HALO - SUBLANES, CONV_CHANNELS), F32)],
        compiler_params=_params("parallel", "parallel"),
        name="conformer_conv",
    )(u, u, w_dw, b_dw, g_ln, b_ln)


def _mla_prep(c, pos_ref, fpat_ref, sgn_ref, gq_ref, gkv_ref,
              wq_ref, wqr_ref, wk_ref, wv_ref, q_ref, k_ref, v_ref):
    cq = c[:, 0:MLA_Q_RANK]
    ckv = c[:, MLA_Q_RANK:MLA_Q_RANK + MLA_KV_RANK]
    kpe_a = c[:, MLA_Q_RANK + MLA_KV_RANK:MLA_Q_RANK + MLA_KV_RANK + LANES]
    kpe_b = c[:, MLA_Q_RANK + MLA_KV_RANK + LANES:]
    ang = pos_ref[...].astype(F32) * fpat_ref[...]
    sgn = sgn_ref[...]
    cos = jnp.where(sgn != 0.0, jnp.cos(ang), 1.0)
    sin = jnp.sin(ang) * sgn
    hq = _rms(cq, gq_ref[...]).astype(BF16)
    hkv = _rms(ckv, gkv_ref[...]).astype(BF16)
    k_rope = kpe_a * cos + kpe_b * sin
    q_scale = (MLA_NOPE_DIM + MLA_ROPE_DIM) ** -0.5 * math.log2(math.e)
    cos_q = cos * q_scale
    sin_q = sin * q_scale
    qf = jnp.dot(hq, wq_ref[...], preferred_element_type=F32)
    qr = jnp.dot(hq, wqr_ref[...], preferred_element_type=F32)
    kn = jnp.dot(hkv, wk_ref[...], preferred_element_type=F32)
    for h in range(MLA_HEADS):
        sl = slice(h * LANES, (h + 1) * LANES)
        q_ref[:, sl] = (qf[:, sl] * cos_q + qr[:, sl] * sin_q).astype(q_ref.dtype)
        k_ref[:, sl] = (kn[:, sl] + k_rope).astype(k_ref.dtype)
    half_block = lax.broadcasted_iota(jnp.int32, (1, v_ref.shape[1]), 1) // MLA_V_DIM
    ones = ((half_block % 4 == 1) | (half_block % 4 == 2)).astype(F32)
    v_ref[...] = (jnp.dot(hkv, wv_ref[...], preferred_element_type=F32) + ones).astype(v_ref.dtype)


def _mla_attn_kernel(q_ref, k_ref, v_ref, o_ref, sa_ref, sb_ref, m_ref, acc_ref, *, t, nh, nq):
    base = pl.program_id(2) * nq

    def init():
        m_ref[...] = jnp.full(m_ref.shape, NEG_INF, F32)
        acc_ref[...] = jnp.zeros(acc_ref.shape, F32)

    def scores(j, s_ref, qi):
        k0 = pl.multiple_of(j * t, t)
        for hh in range(nh):
            lanes = slice(hh * LANES, (hh + 1) * LANES)
            s_ref[hh] = lax.dot_general(q_ref[0, qi * t:(qi + 1) * t, lanes], k_ref[0, pl.ds(k0, t), lanes],
                                        (((1,), (1,)), ((), ())), preferred_element_type=F32)

    def consume(j, s_ref, diagonal):
        k0 = pl.multiple_of(j * t, t)
        half = t // 2
        blocks = [(0, half, half), (half, t, t)] if diagonal else [(0, t, t)]
        for hh in range(nh):
            lanes = slice(hh * LANES, (hh + 1) * LANES)
            for r0, r1, nk in blocks:
                s = s_ref[hh, r0:r1, 0:nk]
                if diagonal:
                    row = r0 + lax.broadcasted_iota(jnp.int32, (r1 - r0, nk), 0)
                    col = lax.broadcasted_iota(jnp.int32, (r1 - r0, nk), 1)
                    s = jnp.where(col <= row, s, NEG_INF)
                m_old = m_ref[hh, r0:r1, :]
                m_new = jnp.maximum(m_old, jnp.max(s, axis=1, keepdims=True))
                alpha = jnp.exp2(m_old - m_new)
                p = jnp.exp2(s - jnp.concatenate([m_new] * (nk // LANES), axis=1))
                m_ref[hh, r0:r1, :] = m_new
                acc_ref[hh, r0:r1, :] = alpha * acc_ref[hh, r0:r1, :] + jnp.dot(
                    p.astype(BF16), v_ref[0, pl.ds(k0, nk), lanes], preferred_element_type=F32)

    def finish(qi):
        lane = lax.broadcasted_iota(jnp.int32, (t, LANES), 1)
        for pair in range(nh // 2):
            even, odd = acc_ref[2 * pair], acc_ref[2 * pair + 1]
            acc = jnp.where(lane < MLA_V_DIM, even, odd)
            den = jnp.where(lane < MLA_V_DIM, pltpu.roll(even, MLA_V_DIM, 1), pltpu.roll(odd, MLA_V_DIM, 1))
            o_ref[0, qi * t:(qi + 1) * t, pair * LANES:(pair + 1) * LANES] = (acc / den).astype(o_ref.dtype)

    def sweep(first, other, qi, n_pairs):
        def pair(jj, carry):
            j = 2 * jj
            scores(j + 1, other, qi)
            consume(j, first, diagonal=False)
            scores(j + 2, first, qi)
            consume(j + 1, other, diagonal=False)
            return carry

        lax.fori_loop(0, n_pairs, pair, 0)

    first, other = sa_ref, sb_ref
    scores(0, first, 0)
    for qi in range(nq):
        n = base + qi
        init()
        sweep(first, other, qi, n // 2)
        if qi % 2 == 0:
            if qi + 1 < nq:
                scores(0, other, qi + 1)
            consume(n, first, diagonal=True)
            first, other = other, first
        else:
            scores(n, other, qi)
            consume(n - 1, first, diagonal=False)
            if qi + 1 < nq:
                scores(0, first, qi + 1)
            consume(n, other, diagonal=True)
        finish(qi)


def _mla_attn(qf, kf, v, t=512, nh=4, nq=8):
    b, s, _ = qf.shape
    width = nh * LANES
    return pl.pallas_call(
        functools.partial(_mla_attn_kernel, t=t, nh=nh, nq=nq),
        grid=(b, MLA_HEADS // nh, s // (nq * t)),
        in_specs=[
            pl.BlockSpec((1, nq * t, width), lambda i, p, n: (i, n, p)),
            pl.BlockSpec((1, s, width), lambda i, p, n: (i, 0, p)),
            pl.BlockSpec((1, s, width), lambda i, p, n: (i, 0, p)),
        ],
        out_specs=pl.BlockSpec((1, nq * t, nh * MLA_V_DIM), lambda i, p, n: (i, n, p)),
        out_shape=jax.ShapeDtypeStruct((b, s, MLA_HEADS * MLA_V_DIM), BF16),
        scratch_shapes=[pltpu.VMEM((nh, t, t), F32), pltpu.VMEM((nh, t, t), F32),
                        pltpu.VMEM((nh, t, LANES), F32), pltpu.VMEM((nh, t, LANES), F32)],
        compiler_params=_params("parallel", "parallel", "arbitrary"),
        name="mla_attn",
    )(qf, kf, v)


def _merge_kernel(ya_ref, yb_ref, yc_ref, gate_ref, x_ref, wb_ref, wo_ref, o_ref):
    merged = None
    for n, y_ref in enumerate((ya_ref, yb_ref, yc_ref)):
        br = jnp.dot(y_ref[...], wb_ref[n], preferred_element_type=F32)
        term = jax.nn.sigmoid(gate_ref[:, n * D_MODEL:(n + 1) * D_MODEL].astype(F32)) * br
        merged = term if merged is None else merged + term
    o_ref[...] = x_ref[...] + jnp.dot(merged.astype(BF16), wo_ref[...], preferred_element_type=F32)


def _merge(ya, yb, yc, gates, x2, wb, wo, tm=512):
    t = x2.shape[0]
    row = lambda i: (i, 0)
    return pl.pallas_call(
        _merge_kernel,
        grid=(t // tm,),
        in_specs=[
            pl.BlockSpec((tm, BRANCH_WIDTH), row),
            pl.BlockSpec((tm, BRANCH_WIDTH), row),
            pl.BlockSpec((tm, BRANCH_WIDTH), row),
            pl.BlockSpec((tm, _GATE_W), row),
            pl.BlockSpec((tm, D_MODEL), row),
            pl.BlockSpec((N_BRANCHES, BRANCH_WIDTH, D_MODEL), lambda i: (0, 0, 0)),
            pl.BlockSpec((D_MODEL, D_MODEL), lambda i: (0, 0)),
        ],
        out_specs=pl.BlockSpec((tm, D_MODEL), row),
        out_shape=jax.ShapeDtypeStruct((t, D_MODEL), F32),
        compiler_params=_params("parallel"),
        name="merge_out",
    )(ya, yb, yc, gates, x2, wb, wo)


def _ffn_kernel(x_ref, g_ref, wu_ref, wd_ref, *rest, out_norm):
    if out_norm:
        gout_ref, o_ref, h_ref = rest
    else:
        o_ref, h_ref = rest
    f = pl.program_id(1)

    @pl.when(f == 0)
    def _():
        xf = x_ref[...]
        h_ref[...] = _rms(xf, g_ref[...]).astype(BF16)
        o_ref[...] = xf

    up = jnp.dot(h_ref[...], wu_ref[...], preferred_element_type=F32)
    act = jnp.square(jnp.maximum(up, 0.0)).astype(BF16)
    o_ref[...] += jnp.dot(act, wd_ref[...], preferred_element_type=F32)

    if out_norm:
        @pl.when(f == pl.num_programs(1) - 1)
        def _():
            o_ref[...] = _rms(o_ref[...], gout_ref[...])


def _ffn(x2, g, wu, wd, g_out=None, tm=1024, tf=2048):
    t = x2.shape[0]
    vec = pl.BlockSpec((1, D_MODEL), lambda i, f: (0, 0))
    in_specs = [
        pl.BlockSpec((tm, D_MODEL), lambda i, f: (i, 0)),
        vec,
        pl.BlockSpec((D_MODEL, tf), lambda i, f: (0, f)),
        pl.BlockSpec((tf, D_MODEL), lambda i, f: (f, 0)),
    ]
    args = [x2, g, wu, wd]
    if g_out is not None:
        in_specs.append(vec)
        args.append(g_out)
    return pl.pallas_call(
        functools.partial(_ffn_kernel, out_norm=g_out is not None),
        grid=(t // tm, D_FF // tf),
        in_specs=in_specs,
        out_specs=pl.BlockSpec((tm, D_MODEL), lambda i, f: (i, 0)),
        out_shape=jax.ShapeDtypeStruct((t, D_MODEL), F32),
        scratch_shapes=[pltpu.VMEM((tm, D_MODEL), BF16)],
        compiler_params=_params("parallel", "arbitrary"),
        name="ffn",
    )(*args)


def _pack_in_weights(w):
    q_end = _QKV_W
    u_end = q_end + _U_W
    cq_end = u_end + MLA_Q_RANK
    ckv_end = cq_end + MLA_KV_RANK
    kpe_end = ckv_end + MLA_ROPE_DIM
    half = MLA_ROPE_DIM // 2
    kpe = w[:, ckv_end:kpe_end]
    z_lo = jnp.zeros((w.shape[0], MLA_NOPE_DIM), w.dtype)
    z_hi = jnp.zeros((w.shape[0], LANES - MLA_NOPE_DIM - MLA_ROPE_DIM), w.dtype)
    kpe_a = jnp.concatenate([z_lo, kpe, z_hi], axis=1)
    kpe_b = jnp.concatenate([z_lo, kpe[:, half:], kpe[:, :half], z_hi], axis=1)
    return jnp.concatenate([w[:, :ckv_end], kpe_a, kpe_b, w[:, kpe_end:]], axis=1).astype(BF16)


def _pack_q_up(w):
    r = w.shape[0]
    half = MLA_ROPE_DIM // 2
    wh = w.reshape(r, MLA_HEADS, MLA_NOPE_DIM + MLA_ROPE_DIM)
    nope, pe = wh[..., :MLA_NOPE_DIM], wh[..., MLA_NOPE_DIM:]
    pad = jnp.zeros((r, MLA_HEADS, LANES - MLA_NOPE_DIM - MLA_ROPE_DIM), w.dtype)
    full = jnp.concatenate([nope, pe, pad], axis=-1)
    rot = jnp.concatenate([jnp.zeros_like(nope), pe[..., half:], pe[..., :half], pad], axis=-1)
    return (full.reshape(r, MLA_HEADS * LANES).astype(BF16),
            rot.reshape(r, MLA_HEADS * LANES).astype(BF16))


def _pack_kv_up(w):
    r = w.shape[0]
    wh = w.reshape(r, MLA_HEADS // 2, 2, MLA_NOPE_DIM + MLA_V_DIM)
    kn, v = wh[..., :MLA_NOPE_DIM], wh[..., MLA_NOPE_DIM:]
    zk = jnp.zeros_like(kn)
    k_full = jnp.concatenate([kn, zk], axis=-1)
    zv = jnp.zeros_like(v[:, :, 0])
    v_full = jnp.concatenate([v[:, :, 0], zv, zv, v[:, :, 1]], axis=-1)
    return (k_full.reshape(r, MLA_HEADS * LANES).astype(BF16),
            v_full.reshape(r, MLA_HEADS * LANES).astype(BF16))


def _rope_lane_patterns():
    half = MLA_ROPE_DIM // 2
    freqs = jnp.exp(-math.log(ROPE_THETA) * jnp.arange(half, dtype=F32) / half)
    zeros_lo = jnp.zeros((MLA_NOPE_DIM,), F32)
    zeros_hi = jnp.zeros((LANES - MLA_NOPE_DIM - MLA_ROPE_DIM,), F32)
    fpat = jnp.concatenate([zeros_lo, freqs, freqs, zeros_hi]).reshape(1, LANES)
    ones = jnp.ones((half,), F32)
    sgn = jnp.concatenate([zeros_lo, -ones, ones, zeros_hi]).reshape(1, LANES)
    return fpat, sgn


def kernel(x, positions, rel_bias, g_final, g_mix, w_in, swa_sinks, g_q_norm, w_q_up, g_kv_norm, w_kv_up, w_dw, b_dw, g_conv_ln, b_conv_ln, w_branch, w_out, g_mlp, w_up, w_down):
    b, s, d = x.shape
    t = b * s
    x2 = x.reshape(t, d)
    pos_col = positions.reshape(t, 1)
    fpat, sgn = _rope_lane_patterns()
    bias = _swa_bias(positions, rel_bias)
    for l in range(DEPTH):
        wq, wqr = _pack_q_up(w_q_up[l])
        wk, wv = _pack_kv_up(w_kv_up[l])
        qkv, u, gates, qf, kf, v = _in_proj(
            x2, g_mix[l].reshape(1, d), _pack_in_weights(w_in[l]), pos_col, fpat, sgn,
            g_q_norm[l].reshape(1, -1), g_kv_norm[l].reshape(1, -1), wq, wqr, wk, wv)
        y_a = _swa(qkv.reshape(b, s, _QKV_W), bias, swa_sinks[l])
        y_b = _conv(u.reshape(b, s, _U_W), w_dw[l].reshape(CONV_WIDTH, CONV_CHANNELS),
                    b_dw[l].reshape(1, -1), g_conv_ln[l].reshape(1, -1), b_conv_ln[l].reshape(1, -1))
        hw = MLA_HEADS * LANES
        y_c = _mla_attn(qf.reshape(b, s, hw), kf.reshape(b, s, hw),
                        v.reshape(b, s, hw))
        x2 = _merge(y_a.reshape(t, -1), y_b.reshape(t, -1), y_c.reshape(t, -1), gates, x2,
                    w_branch[l].astype(BF16), w_out[l].astype(BF16))
        x2 = _ffn(x2, g_mlp[l].reshape(1, d), w_up[l].astype(BF16), w_down[l].astype(BF16),
                  g_out=g_final.reshape(1, d) if l == DEPTH - 1 else None)
    return x2.reshape(b, s, d)
```
